```python
import math
import jax
import jax.numpy as jnp
from jax import lax
import numpy as np

D_MODEL = 4096
BATCH = 4
SEQ = 2048
DEPTH = 4
DEC_BATCH = 8
DEC_SEQ = 4
PAST_LEN = 8192
PAGE_SIZE = 128

N_MIXERS = 3
HEAD_DIM = 128
N_HEADS_A = D_MODEL // HEAD_DIM
Q_BLOCK = 128
S5_GROUP = 16
S5_GROUPS = D_MODEL // S5_GROUP
S5_STATE = 64
C_GROUPS = ((128, 1), (512, 4), (2048, 16))
N_HEADS_C = D_MODEL // HEAD_DIM
D_FF = 4 * D_MODEL
EPS = 1e-6
ATTN_SCALE = HEAD_DIM ** -0.5

kernel_name = 'hybrid_stickbreak_s5_dilated_decode_step'


def _layer_counts():
    kinds = [i % N_MIXERS for i in range(DEPTH)]
    return kinds.count(0), kinds.count(1), kinds.count(2)


def rms_norm(x, g):
    x32 = x.astype(jnp.float32)
    y = x32 * lax.rsqrt(jnp.mean(x32 * x32, axis=-1, keepdims=True) + EPS)
    return (y * g.astype(jnp.float32)).astype(x.dtype)


def sq_relu_mlp(x, w1, w2):
    h = jax.nn.relu(x @ w1)
    return (h * h) @ w2


def _sb_qkv(xn, w_qkv, g_q, g_k):
    b, t, _ = xn.shape
    qkv = (xn @ w_qkv).reshape(b, t, 3, N_HEADS_A, HEAD_DIM)
    return rms_norm(qkv[:, :, 0], g_q), rms_norm(qkv[:, :, 1], g_k), qkv[:, :, 2]


def stick_breaking(q, k, v, bias, q_pos, k_pos):
    z = (jnp.einsum('bqhd,bkhd->bhqk', q, k).astype(jnp.float32) * ATTN_SCALE
         + bias.astype(jnp.float32)[None, :, None, None])
    mask = (k_pos[None, :] < q_pos[:, None])[None, None]
    log_keep = jnp.where(mask, jax.nn.log_sigmoid(-z), 0.0)
    log_later = lax.cumsum(log_keep, axis=3, reverse=True) - log_keep
    w = jnp.where(mask, jnp.exp(jax.nn.log_sigmoid(z) + log_later), 0.0)
    return jnp.einsum('bhqk,bkhd->bqhd', w.astype(v.dtype), v)


def sb_prompt(xn, w_qkv, g_q, g_k, bias, w_o):
    b, t, _ = xn.shape
    q, k, v = _sb_qkv(xn, w_qkv, g_q, g_k)
    nb = t // Q_BLOCK
    pos = jnp.arange(t)
    q_blocks = q.reshape(b, nb, Q_BLOCK, N_HEADS_A, HEAD_DIM).swapaxes(0, 1)
    pos_blocks = pos.reshape(nb, Q_BLOCK)
    o = lax.map(lambda qp: stick_breaking(qp[0], k, v, bias, qp[1], pos), (q_blocks, pos_blocks))
    o = o.swapaxes(0, 1).reshape(b, t, D_MODEL)
    return o @ w_o, k, v


def sb_sample(xn, cache_k, cache_v, page_table, w_qkv, g_q, g_k, bias, w_o):
    b, t, _ = xn.shape
    q, k, v = _sb_qkv(xn, w_qkv, g_q, g_k)
    past = page_table.shape[1] * PAGE_SIZE
    k_past = cache_k[page_table].reshape(b, past, N_HEADS_A, HEAD_DIM).astype(k.dtype)
    v_past = cache_v[page_table].reshape(b, past, N_HEADS_A, HEAD_DIM).astype(v.dtype)
    k_all = jnp.concatenate([k_past, k], axis=1)
    v_all = jnp.concatenate([v_past, v], axis=1)
    o = stick_breaking(q, k_all, v_all, bias, past + jnp.arange(t), jnp.arange(past + t))
    return o.reshape(b, t, D_MODEL) @ w_o, k, v


def s5_discretize(a_re, a_im, log_dt, b_re, b_im):
    f32 = jnp.float32
    dt = jnp.exp(log_dt.astype(f32))[:, None]
    ar = a_re.astype(f32)
    ai = a_im.astype(f32)
    mag = jnp.exp(ar * dt)
    ab_re = mag * jnp.cos(ai * dt)
    ab_im = mag * jnp.sin(ai * dt)
    den = ar * ar + ai * ai
    nr = ab_re - 1.0
    ni = ab_im
    f_re = (nr * ar + ni * ai) / den
    f_im = (ni * ar - nr * ai) / den
    br = b_re.astype(f32)
    bi = b_im.astype(f32)
    bb_re = f_re[..., None] * br - f_im[..., None] * bi
    bb_im = f_re[..., None] * bi + f_im[..., None] * br
    return ab_re, ab_im, bb_re, bb_im


def _complex_combine(e1, e2):
    ar1, ai1, br1, bi1 = e1
    ar2, ai2, br2, bi2 = e2
    return (ar2 * ar1 - ai2 * ai1,
            ar2 * ai1 + ai2 * ar1,
            ar2 * br1 - ai2 * bi1 + br2,
            ar2 * bi1 + ai2 * br1 + bi2)


def s5_mixer(xn, h0_re, h0_im, w_in, a_re, a_im, log_dt, b_re, b_im, c_re, c_im, d_skip, w_glu, w_gate):
    f32 = jnp.float32
    bsz, t, _ = xn.shape
    u = (xn @ w_in).reshape(bsz, t, S5_GROUPS, S5_GROUP).astype(f32)
    ab_re, ab_im, bb_re, bb_im = s5_discretize(a_re, a_im, log_dt, b_re, b_im)
    bu_re = jnp.einsum('btgc,gpc->btgp', u, bb_re)
    bu_im = jnp.einsum('btgc,gpc->btgp', u, bb_im)
    if h0_re is not None:
        h0r = h0_re.astype(f32)
        h0i = h0_im.astype(f32)
        bu_re = bu_re.at[:, 0].add(ab_re * h0r - ab_im * h0i)
        bu_im = bu_im.at[:, 0].add(ab_re * h0i + ab_im * h0r)
    shape = (1, t) + ab_re.shape
    _, _, h_re, h_im = lax.associative_scan(
        _complex_combine,
        (jnp.broadcast_to(ab_re, shape), jnp.broadcast_to(ab_im, shape), bu_re, bu_im),
        axis=1)
    y = (jnp.einsum('btgp,gcp->btgc', h_re, c_re.astype(f32))
         - jnp.einsum('btgp,gcp->btgc', h_im, c_im.astype(f32))
         + d_skip.astype(f32) * u)
    y = jax.nn.gelu(y.reshape(bsz, t, D_MODEL)).astype(xn.dtype)
    out = (y @ w_glu) * jax.nn.sigmoid(y @ w_gate)
    return out, h_re[:, -1], h_im[:, -1]


def _c_qkv(xn, w_qkv, g_q, g_k):
    b, t, _ = xn.shape
    qkv = (xn @ w_qkv).reshape(b, t, len(C_GROUPS), 3, N_HEADS_C, HEAD_DIM)
    q = rms_norm(qkv[:, :, :, 0], g_q[:, None, :])
    k = rms_norm(qkv[:, :, :, 1], g_k[:, None, :])
    return q, k, qkv[:, :, :, 2]


def band_attention(q, k, v, span):
    n, L, h, dh = q.shape
    nb = -(-L // span)
    lp = nb * span
    pad = ((0, 0), (0, lp - L), (0, 0), (0, 0))
    qb = jnp.pad(q, pad).reshape(n, nb, span, h, dh)
    kb = jnp.pad(k, pad).reshape(n, nb, span, h, dh)
    vb = jnp.pad(v, pad).reshape(n, nb, span, h, dh)
    prev = ((0, 0), (1, 0), (0, 0), (0, 0), (0, 0))
    kk = jnp.concatenate([jnp.pad(kb, prev)[:, :-1], kb], axis=2)
    vv = jnp.concatenate([jnp.pad(vb, prev)[:, :-1], vb], axis=2)
    s = jnp.einsum('nbqhd,nbkhd->nbhqk', qb, kk).astype(jnp.float32) * ATTN_SCALE
    a = jnp.arange(span)[:, None]
    c = jnp.arange(2 * span)[None, :]
    band = (c >= a) & (c <= a + span)
    has_prev = (jnp.arange(nb)[:, None, None] > 0) | (c[None] >= span)
    mask = (band[None] & has_prev)[None, :, None]
    s = jnp.where(mask, s, -jnp.inf)
    lse = jax.nn.logsumexp(s, axis=-1)
    p = jnp.exp(s - lse[..., None])
    o = jnp.einsum('nbhqk,nbkhd->nbqhd', p.astype(vv.dtype), vv).reshape(n, lp, h, dh)[:, :L]
    lse = lse.transpose(0, 1, 3, 2).reshape(n, lp, h)[:, :L]
    return o, lse


def _merge_groups(outs, lses):
    wts = jax.nn.softmax(jnp.stack(lses, 0), axis=0)
    return jnp.sum(wts[..., None] * jnp.stack(outs, 0).astype(jnp.float32), axis=0)


def dilated_prompt(xn, w_qkv, g_q, g_k, w_o):
    b, t, _ = xn.shape
    q, k, v = _c_qkv(xn, w_qkv, g_q, g_k)
    outs, lses, new_k, new_v = [], [], [], []
    for gi, (win, dil) in enumerate(C_GROUPS):
        span = win // dil
        to_res = lambda x: x.reshape((b, t // dil, dil) + x.shape[2:]).swapaxes(1, 2).reshape((b * dil, t // dil) + x.shape[2:])
        from_res = lambda x: x.reshape((b, dil, t // dil) + x.shape[2:]).swapaxes(1, 2).reshape((b, t) + x.shape[2:])
        o, lse = band_attention(to_res(q[:, :, gi]), to_res(k[:, :, gi]), to_res(v[:, :, gi]), span)
        outs.append(from_res(o))
        lses.append(from_res(lse))
        keep = min(win, t)
        new_k.append(k[:, t - keep:, gi])
        new_v.append(v[:, t - keep:, gi])
    o = _merge_groups(outs, lses).astype(xn.dtype).reshape(b, t, N_HEADS_C * HEAD_DIM)
    return o @ w_o, new_k, new_v


def dilated_sample(xn, bufs_k, bufs_v, w_qkv, g_q, g_k, w_o):
    b, t, _ = xn.shape
    q, k, v = _c_qkv(xn, w_qkv, g_q, g_k)
    outs, lses, new_k, new_v = [], [], [], []
    for gi, (win, dil) in enumerate(C_GROUPS):
        span = win // dil
        wb = bufs_k[gi].shape[1]
        k_all = jnp.concatenate([bufs_k[gi].astype(k.dtype), k[:, :, gi]], axis=1)
        v_all = jnp.concatenate([bufs_v[gi].astype(v.dtype), v[:, :, gi]], axis=1)
        idx = wb + jnp.arange(t)[:, None] - dil * jnp.arange(span + 1)[None, :]
        valid = idx >= 0
        idx = jnp.maximum(idx, 0)
        kg = k_all[:, idx]
        vg = v_all[:, idx]
        s = jnp.einsum('bqhd,bqmhd->bqhm', q[:, :, gi], kg).astype(jnp.float32) * ATTN_SCALE
        s = jnp.where(valid[None, :, None, :], s, -jnp.inf)
        lse = jax.nn.logsumexp(s, axis=-1)
        p = jnp.exp(s - lse[..., None])
        outs.append(jnp.einsum('bqhm,bqmhd->bqhd', p.astype(vg.dtype), vg))
        lses.append(lse)
        new_k.append(k_all[:, -wb:])
        new_v.append(v_all[:, -wb:])
    o = _merge_groups(outs, lses).astype(xn.dtype).reshape(b, t, N_HEADS_C * HEAD_DIM)
    return o @ w_o, new_k, new_v


def setup_inputs(seed: int = 0) -> dict:
    n_a, n_b, n_c = _layer_counts()
    n_pages = PAST_LEN // PAGE_SIZE
    n_used = DEC_BATCH * n_pages
    n_pool = n_used + max(1, n_used // 4)
    ks = iter(jax.random.split(jax.random.key(seed), 48))
    nrm = lambda shape, scale=1.0: jax.random.normal(next(ks), shape, jnp.float32) * scale
    gain = lambda shape: 1.0 + nrm(shape, 0.05)
    D = D_MODEL
    inp = {}
    inp['x_prompt'] = nrm((BATCH, SEQ, D))
    inp['x_sample'] = nrm((DEC_BATCH, DEC_SEQ, D))
    a_shape = (n_a, n_pool, PAGE_SIZE, N_HEADS_A, HEAD_DIM)
    inp['cache_a_k'] = nrm(a_shape)
    inp['cache_a_v'] = nrm(a_shape)
    inp['state_b_re'] = nrm((n_b, DEC_BATCH, S5_GROUPS, S5_STATE), 0.3)
    inp['state_b_im'] = nrm((n_b, DEC_BATCH, S5_GROUPS, S5_STATE), 0.3)
    for win, _ in C_GROUPS:
        shp = (n_c, DEC_BATCH, min(win, PAST_LEN), N_HEADS_C, HEAD_DIM)
        inp['cache_c_k_w%d' % win] = nrm(shp)
        inp['cache_c_v_w%d' % win] = nrm(shp)
    inp['page_table'] = jax.random.permutation(next(ks), n_pool)[:n_used].reshape(DEC_BATCH, n_pages).astype(jnp.int32)
    inp['norm_mix'] = gain((DEPTH, D))
    inp['norm_ffn'] = gain((DEPTH, D))
    inp['a_w_qkv'] = nrm((n_a, D, 3 * D), D ** -0.5)
    inp['a_q_gain'] = gain((n_a, HEAD_DIM))
    inp['a_k_gain'] = gain((n_a, HEAD_DIM))
    inp['a_logit_bias'] = -7.0 - 3.0 * jax.random.uniform(next(ks), (n_a, N_HEADS_A), jnp.float32)
    inp['a_w_o'] = nrm((n_a, D, D), D ** -0.5)
    inp['b_w_in'] = nrm((n_b, D, D), D ** -0.5)
    inp['b_a_re'] = -0.5 + nrm((n_b, S5_GROUPS, S5_STATE), 0.01)
    inp['b_a_im'] = jnp.pi * jnp.arange(S5_STATE, dtype=jnp.float32) + nrm((n_b, S5_GROUPS, S5_STATE), 0.01)
    inp['b_log_dt'] = jax.random.uniform(next(ks), (n_b, S5_GROUPS), jnp.float32, math.log(1e-3), math.log(1e-1))
    inp['b_b_re'] = nrm((n_b, S5_GROUPS, S5_STATE, S5_GROUP), (2 * S5_GROUP) ** -0.5)
    inp['b_b_im'] = nrm((n_b, S5_GROUPS, S5_STATE, S5_GROUP), (2 * S5_GROUP) ** -0.5)
    inp['b_c_re'] = nrm((n_b, S5_GROUPS, S5_GROUP, S5_STATE), S5_STATE ** -0.5)
    inp['b_c_im'] = nrm((n_b, S5_GROUPS, S5_GROUP, S5_STATE), S5_STATE ** -0.5)
    inp['b_d'] = nrm((n_b, S5_GROUPS, S5_GROUP))
    inp['b_w_glu'] = nrm((n_b, D, D), D ** -0.5)
    inp['b_w_gate'] = nrm((n_b, D, D), D ** -0.5)
    inp['c_w_qkv'] = nrm((n_c, D, len(C_GROUPS) * 3 * N_HEADS_C * HEAD_DIM), D ** -0.5)
    inp['c_q_gain'] = gain((n_c, len(C_GROUPS), HEAD_DIM))
    inp['c_k_gain'] = gain((n_c, len(C_GROUPS), HEAD_DIM))
    inp['c_w_o'] = nrm((n_c, N_HEADS_C * HEAD_DIM, D), (N_HEADS_C * HEAD_DIM) ** -0.5)
    inp['ffn_w1'] = nrm((DEPTH, D, D_FF), D ** -0.5)
    inp['ffn_w2'] = nrm((DEPTH, D_FF, D), D_FF ** -0.5)
    return inp


def reference(x_prompt, x_sample, cache_a_k, cache_a_v, state_b_re, state_b_im,
              cache_c_k_w128, cache_c_v_w128, cache_c_k_w512, cache_c_v_w512,
              cache_c_k_w2048, cache_c_v_w2048, page_table, norm_mix, norm_ffn,
              a_w_qkv, a_q_gain, a_k_gain, a_logit_bias, a_w_o,
              b_w_in, b_a_re, b_a_im, b_log_dt, b_b_re, b_b_im, b_c_re, b_c_im, b_d, b_w_glu, b_w_gate,
              c_w_qkv, c_q_gain, c_k_gain, c_w_o, ffn_w1, ffn_w2):
    c_buf_k = (cache_c_k_w128, cache_c_k_w512, cache_c_k_w2048)
    c_buf_v = (cache_c_v_w128, cache_c_v_w512, cache_c_v_w2048)
    ak_p, av_p, ak_s, av_s = [], [], [], []
    br_p, bi_p, br_s, bi_s = [], [], [], []
    ck_p = [[] for _ in C_GROUPS]
    cv_p = [[] for _ in C_GROUPS]
    ck_s = [[] for _ in C_GROUPS]
    cv_s = [[] for _ in C_GROUPS]
    xp, xs = x_prompt, x_sample
    ia = ib = ic = 0
    for layer in range(DEPTH):
        hp = rms_norm(xp, norm_mix[layer])
        hs = rms_norm(xs, norm_mix[layer])
        kind = layer % N_MIXERS
        if kind == 0:
            ap = (a_w_qkv[ia], a_q_gain[ia], a_k_gain[ia], a_logit_bias[ia], a_w_o[ia])
            mp, kp, vp = sb_prompt(hp, *ap)
            ms, kn, vn = sb_sample(hs, cache_a_k[ia], cache_a_v[ia], page_table, *ap)
            ak_p.append(kp); av_p.append(vp); ak_s.append(kn); av_s.append(vn)
            ia += 1
        elif kind == 1:
            bp = (b_w_in[ib], b_a_re[ib], b_a_im[ib], b_log_dt[ib], b_b_re[ib], b_b_im[ib],
                  b_c_re[ib], b_c_im[ib], b_d[ib], b_w_glu[ib], b_w_gate[ib])
            mp, rp, ip = s5_mixer(hp, None, None, *bp)
            ms, rn, im_ = s5_mixer(hs, state_b_re[ib], state_b_im[ib], *bp)
            br_p.append(rp); bi_p.append(ip); br_s.append(rn); bi_s.append(im_)
            ib += 1
        else:
            cp = (c_w_qkv[ic], c_q_gain[ic], c_k_gain[ic], c_w_o[ic])
            mp, kps, vps = dilated_prompt(hp, *cp)
            ms, kss, vss = dilated_sample(hs, [bk[ic] for bk in c_buf_k], [bv[ic] for bv in c_buf_v], *cp)
            for gi in range(len(C_GROUPS)):
                ck_p[gi].append(kps[gi]); cv_p[gi].append(vps[gi])
                ck_s[gi].append(kss[gi]); cv_s[gi].append(vss[gi])
            ic += 1
        xp = xp + mp
        xs = xs + ms
        xp = xp + sq_relu_mlp(rms_norm(xp, norm_ffn[layer]), ffn_w1[layer], ffn_w2[layer])
        xs = xs + sq_relu_mlp(rms_norm(xs, norm_ffn[layer]), ffn_w1[layer], ffn_w2[layer])
    return (xp, xs,
            jnp.stack(ak_p), jnp.stack(av_p), jnp.stack(ak_s), jnp.stack(av_s),
            jnp.stack(br_p), jnp.stack(bi_p), jnp.stack(br_s), jnp.stack(bi_s),
            jnp.stack(ck_p[0]), jnp.stack(cv_p[0]), jnp.stack(ck_p[1]), jnp.stack(cv_p[1]),
            jnp.stack(ck_p[2]), jnp.stack(cv_p[2]),
            jnp.stack(ck_s[0]), jnp.stack(cv_s[0]), jnp.stack(ck_s[1]), jnp.stack(cv_s[1]),
            jnp.stack(ck_s[2]), jnp.stack(cv_s[2]))
```

```python
import functools
import math

import jax
import jax.numpy as jnp
from jax import lax
from jax.experimental import pallas as pl
from jax.experimental.pallas import tpu as pltpu

F32 = jnp.float32
BF16 = jnp.bfloat16

HEAD_DIM = 128
KEY_BLOCK = 128
EPS = 1e-6
ATTN_SCALE = HEAD_DIM ** -0.5
N_MIXERS = 3
S5_GROUP = 16
S5_STATE = 64
S5_TILE_GROUPS = HEAD_DIM // S5_GROUP
S5_TILE_STATE = S5_TILE_GROUPS * S5_STATE
S5_CHUNK = 16
C_GROUPS = ((128, 1), (512, 4), (2048, 16))
VMEM_CAP = 60 * 1024 * 1024
NEG_BIG = -1e30

_NT = (((1,), (1,)), ((), ()))


def _params(sem, vmem_bytes=None):
    limit = None if vmem_bytes is None else int(min(VMEM_CAP, vmem_bytes))
    return pltpu.CompilerParams(dimension_semantics=sem, vmem_limit_bytes=limit)


def _rmsnorm_kernel(x_ref, g_ref, o_ref):
    x = x_ref[...]
    ms = jnp.mean(x * x, axis=-1, keepdims=True)
    o_ref[...] = (x * lax.rsqrt(ms + EPS) * g_ref[...]).astype(o_ref.dtype)


def rmsnorm(x, gains, layer):
    m, d = x.shape
    bm = min(m, 256)
    g3 = gains.reshape(gains.shape[0], 1, d)
    return pl.pallas_call(
        _rmsnorm_kernel,
        out_shape=jax.ShapeDtypeStruct((m, d), BF16),
        grid=(m // bm,),
        in_specs=[pl.BlockSpec((bm, d), lambda i: (i, 0)),
                  pl.BlockSpec((None, 1, d), lambda i: (layer, 0, 0))],
        out_specs=pl.BlockSpec((bm, d), lambda i: (i, 0)),
        compiler_params=_params(("parallel",)),
        name="rmsnorm",
    )(x, g3)


def _mm_kernel(*refs, n_w, epi, has_res, sec_blocks):
    it = iter(refs)
    x_ref = next(it)
    w_refs = [next(it) for _ in range(n_w)]
    gain_ref = next(it) if epi == "headnorm" else None
    res_ref = next(it) if has_res else None
    o_ref = next(it)
    wbf = [next(it) for _ in range(n_w)]
    j = pl.program_id(0)
    i = pl.program_id(1)

    @pl.when(i == 0)
    def _():
        for w_ref, s in zip(w_refs, wbf):
            s[...] = w_ref[...].astype(BF16)

    x = x_ref[...]
    acc = jnp.dot(x, wbf[0][...], preferred_element_type=F32)
    if epi == "glu":
        acc = acc * jax.nn.sigmoid(jnp.dot(x, wbf[1][...], preferred_element_type=F32))
    elif epi == "relu2":
        r = jnp.maximum(acc, 0.0)
        acc = r * r
    if epi == "headnorm":
        sec = (j // sec_blocks) % 3

        @pl.when(sec < 2)
        def _():
            for c in range(acc.shape[1] // HEAD_DIM):
                sl = slice(c * HEAD_DIM, (c + 1) * HEAD_DIM)
                y = acc[:, sl]
                ms = jnp.mean(y * y, axis=-1, keepdims=True)
                o_ref[:, sl] = (y * lax.rsqrt(ms + EPS) * gain_ref[:, sl]).astype(o_ref.dtype)

        @pl.when(sec == 2)
        def _():
            o_ref[...] = acc.astype(o_ref.dtype)
    else:
        if has_res:
            acc = res_ref[...] + acc
        o_ref[...] = acc.astype(o_ref.dtype)


def matmul(x, ws, layer, n_out, *, epi="none", res=None, gain=None, out_dtype=F32, bn=512):
    m, k = x.shape
    n_w = len(ws)
    bm = min(m, 1024)
    bn = math.gcd(bn, n_out, k if epi == "headnorm" else n_out)
    assert m % bm == 0 and bn % HEAD_DIM == 0
    grid = (n_out // bn, m // bm)
    in_specs = [pl.BlockSpec((bm, k), lambda j, i: (i, 0))]
    in_specs += [pl.BlockSpec((None, k, bn), lambda j, i: (layer, 0, j)) for _ in ws]
    args = [x, *ws]
    sec_blocks = 1
    if epi == "headnorm":
        assert k % bn == 0
        sec_blocks = k // bn
        in_specs.append(pl.BlockSpec((1, bn), lambda j, i: (0, j)))
        args.append(gain)
    if res is not None:
        in_specs.append(pl.BlockSpec((bm, bn), lambda j, i: (i, j)))
        args.append(res)
    out_bytes = jnp.dtype(out_dtype).itemsize
    vmem = (2 * bm * k * 2 + n_w * (2 * k * bn * 4 + k * bn * 2)
            + 2 * bm * bn * (out_bytes + (4 if res is not None else 0))
            + (2 + n_w) * bm * bn * 4 + (4 << 20))
    return pl.pallas_call(
        functools.partial(_mm_kernel, n_w=n_w, epi=epi, has_res=res is not None,
                          sec_blocks=sec_blocks),
        out_shape=jax.ShapeDtypeStruct((m, n_out), out_dtype),
        grid=grid,
        in_specs=in_specs,
        out_specs=pl.BlockSpec((bm, bn), lambda j, i: (i, j)),
        scratch_shapes=[pltpu.VMEM((k, bn), BF16) for _ in ws],
        compiler_params=_params(("parallel", "arbitrary"), vmem),
        name="mm_" + epi,
    )(*args)


def _mm_kt_kernel(x_ref, w_ref, res_ref, o_ref, acc_ref, *, nk):
    kk = pl.program_id(2)

    @pl.when(kk == 0)
    def _():
        acc_ref[...] = jnp.zeros_like(acc_ref)

    acc_ref[...] += jnp.dot(x_ref[...], w_ref[...].astype(BF16), preferred_element_type=F32)

    @pl.when(kk == nk - 1)
    def _():
        o_ref[...] = res_ref[...] + acc_ref[...]


def matmul_ktiled(x, w, layer, res, *, bn=1024, bk=2048):
    m, k = x.shape
    n = w.shape[-1]
    bm = min(m, 1024)
    bn = min(bn, n)
    bk = min(bk, k)
    assert m % bm == 0 and n % bn == 0 and k % bk == 0
    nk = k // bk
    vmem = 2 * bm * bk * 2 + 2 * bk * bn * 4 + bk * bn * 2 + 5 * bm * bn * 4 + (4 << 20)
    return pl.pallas_call(
        functools.partial(_mm_kt_kernel, nk=nk),
        out_shape=jax.ShapeDtypeStruct((m, n), F32),
        grid=(m // bm, n // bn, nk),
        in_specs=[pl.BlockSpec((bm, bk), lambda i, j, kk: (i, kk)),
                  pl.BlockSpec((None, bk, bn), lambda i, j, kk: (layer, kk, j)),
                  pl.BlockSpec((bm, bn), lambda i, j, kk: (i, j))],
        out_specs=pl.BlockSpec((bm, bn), lambda i, j, kk: (i, j)),
        scratch_shapes=[pltpu.VMEM((bm, bn), F32)],
        compiler_params=_params(("parallel", "parallel", "arbitrary"), vmem),
        name="mm_ktiled",
    )(x, w, res)


def _cumsum_matrix():
    jj = lax.broadcasted_iota(jnp.int32, (KEY_BLOCK, 2 * KEY_BLOCK), 0)
    ss = lax.broadcasted_iota(jnp.int32, (KEY_BLOCK, 2 * KEY_BLOCK), 1)
    return jnp.where((ss >= KEY_BLOCK) | (jj > ss), 1.0, 0.0).astype(BF16)


def _log_keep(z):
    return -(jnp.maximum(z, 0.0) + jnp.log1p(jnp.exp(-jnp.abs(z))))


def _stick_block(z, valid, run, cs):
    lk = _log_keep(z)
    if valid is not None:
        lk = jnp.where(valid, lk, 0.0)
    hi = lk.astype(BF16)
    lo = (lk - hi.astype(F32)).astype(BF16)
    sums = (jnp.dot(hi, cs, preferred_element_type=F32)
            + jnp.dot(lo, cs, preferred_element_type=F32))
    w = jnp.exp(z + lk + sums[:, :KEY_BLOCK] + run)
    if valid is not None:
        w = jnp.where(valid, w, 0.0)
    return w, run + sums[:, KEY_BLOCK:]


def _sb_prompt_kernel(bias_ref, q_ref, k_ref, v_ref, o_ref, *, bq):
    h = pl.program_id(1)
    qi = pl.program_id(2)
    nd = bq // KEY_BLOCK
    bias = bias_ref[h]
    q = q_ref[...].astype(BF16)
    cs = _cumsum_matrix()
    row = lax.broadcasted_iota(jnp.int32, (bq, KEY_BLOCK), 0)
    col = lax.broadcasted_iota(jnp.int32, (bq, KEY_BLOCK), 1)

    def block(j, run, acc, diag_off):
        ks = pl.multiple_of(j * KEY_BLOCK, KEY_BLOCK)
        kb = k_ref[pl.ds(ks, KEY_BLOCK), :].astype(BF16)
        vb = v_ref[pl.ds(ks, KEY_BLOCK), :].astype(BF16)
        z = lax.dot_general(q, kb, _NT, preferred_element_type=F32) * ATTN_SCALE + bias
        valid = None if diag_off is None else (col + diag_off) < row
        w, run = _stick_block(z, valid, run, cs)
        acc = acc + jnp.dot(w.astype(BF16), vb, preferred_element_type=F32)
        return run, acc

    run = jnp.zeros((bq, KEY_BLOCK), F32)
    acc = jnp.zeros((bq, HEAD_DIM), F32)
    for dd in reversed(range(nd)):
        run, acc = block(qi * nd + dd, run, acc, dd * KEY_BLOCK)

    def body(it, carry):
        return block(qi * nd - 1 - it, carry[0], carry[1], None)

    run, acc = lax.fori_loop(0, qi * nd, body, (run, acc))
    o_ref[...] = acc.astype(o_ref.dtype)


def sb_prompt_attention(qkv, bias, batch, seq, heads):
    bq = min(seq, 256)
    nq = seq // bq
    d = heads * HEAD_DIM
    return pl.pallas_call(
        functools.partial(_sb_prompt_kernel, bq=bq),
        out_shape=jax.ShapeDtypeStruct((batch * seq, d), BF16),
        grid=(batch, heads, nq),
        in_specs=[pl.BlockSpec(memory_space=pltpu.SMEM),
                  pl.BlockSpec((bq, HEAD_DIM), lambda b, h, i: (b * nq + i, h)),
                  pl.BlockSpec((seq, HEAD_DIM), lambda b, h, i: (b, heads + h)),
                  pl.BlockSpec((seq, HEAD_DIM), lambda b, h, i: (b, 2 * heads + h))],
        out_specs=pl.BlockSpec((bq, HEAD_DIM), lambda b, h, i: (b * nq + i, h)),
        compiler_params=_params(("parallel", "parallel", "arbitrary")),
        name="sb_prompt",
    )(bias, qkv, qkv, qkv)


def _head_diag_mask(heads):
    d = heads * HEAD_DIM
    return (lax.broadcasted_iota(jnp.int32, (heads, d), 1) // HEAD_DIM
            == lax.broadcasted_iota(jnp.int32, (heads, d), 0))


def _fill_block_diag_q(qbd, q, heads, tq):
    m = _head_diag_mask(heads)
    d = heads * HEAD_DIM
    for t in range(tq):
        rows = jnp.broadcast_to(q[t:t + 1, :], (heads, d))
        qbd[t * heads:(t + 1) * heads, :] = jnp.where(m, rows, 0.0).astype(qbd.dtype)


def _take_head_diag(a, heads, tq):
    m = _head_diag_mask(heads)
    return [jnp.sum(jnp.where(m, a[t * heads:(t + 1) * heads, :], 0.0), axis=0, keepdims=True)
            for t in range(tq)]


def _sb_sample_kernel(pt_ref, bias_ref, qkv_ref, kc_ref, vc_ref, o_ref,
                      qbd, kn, vn, acc, run_ref, *, heads, tq, n_pages):
    del pt_ref
    p = pl.program_id(1)
    d = heads * HEAD_DIM
    rows = tq * heads
    cs = _cumsum_matrix()

    def process(kb, vb, valid):
        z = (lax.dot_general(qbd[...], kb, _NT, preferred_element_type=F32) * ATTN_SCALE
             + bias_ref[...])
        w, run = _stick_block(z, valid, run_ref[...], cs)
        run_ref[...] = run
        acc[...] += jnp.dot(w.astype(BF16), vb, preferred_element_type=F32)

    @pl.when(p == 0)
    def _():
        _fill_block_diag_q(qbd, qkv_ref[:, 0:d], heads, tq)
        kn[...] = jnp.zeros_like(kn)
        vn[...] = jnp.zeros_like(vn)
        kn[0:tq, :] = qkv_ref[:, d:2 * d]
        vn[0:tq, :] = qkv_ref[:, 2 * d:3 * d]
        acc[...] = jnp.zeros_like(acc)
        run_ref[...] = jnp.zeros_like(run_ref)
        tok = lax.broadcasted_iota(jnp.int32, (rows, KEY_BLOCK), 0) // heads
        key = lax.broadcasted_iota(jnp.int32, (rows, KEY_BLOCK), 1)
        process(kn[...].astype(BF16), vn[...].astype(BF16), key < tok)

    @pl.when(p > 0)
    def _():
        process(kc_ref[...].astype(BF16), vc_ref[...].astype(BF16), None)

    @pl.when(p == n_pages)
    def _():
        for t, r in enumerate(_take_head_diag(acc[...], heads, tq)):
            o_ref[t:t + 1, :] = r.astype(o_ref.dtype)


def sb_sample_attention(qkv3, cache_k, cache_v, layer, page_table, bias):
    b, tq, d3 = qkv3.shape
    d = d3 // 3
    heads = d // HEAD_DIM
    rows = tq * heads
    n_pages = page_table.shape[1]
    na, pool, page, _, _ = cache_k.shape
    assert page == KEY_BLOCK and tq <= KEY_BLOCK
    ck = cache_k.reshape(na, pool, page, d)
    cv = cache_v.reshape(na, pool, page, d)
    bias_rows = jnp.broadcast_to(jnp.tile(bias.astype(F32), tq)[:, None], (rows, KEY_BLOCK))

    def page_map(bb, p, pt):
        return (layer, pt[bb, n_pages - jnp.maximum(p, 1)], 0, 0)

    grid_spec = pltpu.PrefetchScalarGridSpec(
        num_scalar_prefetch=1,
        grid=(b, n_pages + 1),
        in_specs=[pl.BlockSpec((rows, KEY_BLOCK), lambda bb, p, pt: (0, 0)),
                  pl.BlockSpec((None, tq, d3), lambda bb, p, pt: (bb, 0, 0)),
                  pl.BlockSpec((None, None, page, d), page_map),
                  pl.BlockSpec((None, None, page, d), page_map)],
        out_specs=pl.BlockSpec((None, tq, d), lambda bb, p, pt: (bb, 0, 0)),
        scratch_shapes=[pltpu.VMEM((rows, d), BF16),
                        pltpu.VMEM((KEY_BLOCK, d), F32),
                        pltpu.VMEM((KEY_BLOCK, d), F32),
                        pltpu.VMEM((rows, d), F32),
                        pltpu.VMEM((rows, KEY_BLOCK), F32)])
    return pl.pallas_call(
        functools.partial(_sb_sample_kernel, heads=heads, tq=tq, n_pages=n_pages),
        out_shape=jax.ShapeDtypeStruct((b, tq, d), F32),
        grid_spec=grid_spec,
        compiler_params=_params(("parallel", "arbitrary"), 40 << 20),
        name="sb_sample",
    )(page_table, bias_rows, qkv3, ck, cv)


def _s5_discretize(lam_re, lam_im, log_dt):
    dt = jnp.exp(log_dt)
    mag = jnp.exp(lam_re * dt)
    ab_re = mag * jnp.cos(lam_im * dt)
    ab_im = mag * jnp.sin(lam_im * dt)
    den = lam_re * lam_re + lam_im * lam_im
    nr = ab_re - 1.0
    ni = ab_im
    f_re = (nr * lam_re + ni * lam_im) / den
    f_im = (ni * lam_re - nr * lam_im) / den
    return ab_re, ab_im, f_re, f_im


def _s5_tile_mask():
    r = lax.broadcasted_iota(jnp.int32, (HEAD_DIM, S5_TILE_STATE), 0) // S5_GROUP
    c = lax.broadcasted_iota(jnp.int32, (HEAD_DIM, S5_TILE_STATE), 1) // S5_STATE
    return r == c


def _s5_bbar(bt_re, bt_im, f_re, f_im):
    m = _s5_tile_mask()
    bb_re = jnp.where(m, f_re * bt_re - f_im * bt_im, 0.0)
    bb_im = jnp.where(m, f_re * bt_im + f_im * bt_re, 0.0)
    return bb_re, bb_im


def _gelu_tanh(x):
    return 0.5 * x * (1.0 + jnp.tanh(0.7978845608028654 * (x + 0.044715 * (x * x * x))))


def _s5_prompt_kernel(u_ref, lre_ref, lim_ref, ldt_ref, btr_ref, bti_ref, cr_ref, ci_ref, d_ref,
                      y_ref, hre_ref, him_ref,
                      toep, bend, cpow, ucat, sloc, hprev, yacc, ys, *, batch, seq):
    nch = seq // S5_CHUNK
    ns = S5_TILE_STATE
    ab_re, ab_im, f_re, f_im = _s5_discretize(lre_ref[...], lim_ref[...], ldt_ref[...])
    bb_re, bb_im = _s5_bbar(btr_ref[...], bti_ref[...], f_re, f_im)
    m = _s5_tile_mask()
    c_re = jnp.where(m, cr_ref[...], 0.0)
    c_im = jnp.where(m, ci_ref[...], 0.0)
    c_cat = jnp.concatenate([c_re, -c_im], axis=1).astype(BF16)

    toep[...] = jnp.zeros_like(toep)
    p_re = jnp.ones_like(ab_re)
    p_im = jnp.zeros_like(ab_im)
    for tau in range(S5_CHUNK):
        bt_cat = jnp.concatenate([bb_re * p_re - bb_im * p_im, bb_re * p_im + bb_im * p_re],
                                 axis=1).astype(BF16)
        s_blk = S5_CHUNK - 1 - tau
        bend[s_blk * HEAD_DIM:(s_blk + 1) * HEAD_DIM, :] = bt_cat
        k_tau = lax.dot_general(bt_cat, c_cat, _NT, preferred_element_type=F32).astype(BF16)
        for s in range(S5_CHUNK - tau):
            t = s + tau
            toep[s * HEAD_DIM:(s + 1) * HEAD_DIM, t * HEAD_DIM:(t + 1) * HEAD_DIM] = k_tau
        p_re, p_im = p_re * ab_re - p_im * ab_im, p_re * ab_im + p_im * ab_re
        cpow[tau * HEAD_DIM:(tau + 1) * HEAD_DIM, :] = jnp.concatenate(
            [c_re * p_re - c_im * p_im, -(c_re * p_im + c_im * p_re)], axis=1).astype(BF16)
    a16_re, a16_im = p_re, p_im

    for b in range(batch):
        for s in range(S5_CHUNK):
            ucat[b * nch:(b + 1) * nch, s * HEAD_DIM:(s + 1) * HEAD_DIM] = (
                u_ref[b, pl.ds(s, nch, stride=S5_CHUNK), :].astype(BF16))
    uc = ucat[...]
    yacc[...] = jnp.dot(uc, toep[...], preferred_element_type=F32)
    sloc[...] = jnp.dot(uc, bend[...], preferred_element_type=F32)

    def step(n, carry):
        out = []
        for b in range(batch):
            h_re, h_im = carry[2 * b], carry[2 * b + 1]
            r = b * nch + n
            hprev[pl.ds(r, 1), 0:ns] = h_re
            hprev[pl.ds(r, 1), ns:2 * ns] = h_im
            s_re = sloc[pl.ds(r, 1), 0:ns]
            s_im = sloc[pl.ds(r, 1), ns:2 * ns]
            out.append(a16_re * h_re - a16_im * h_im + s_re)
            out.append(a16_re * h_im + a16_im * h_re + s_im)
        return tuple(out)

    zero = jnp.zeros((1, ns), F32)
    fin = lax.fori_loop(0, nch, step, (zero,) * (2 * batch))
    for b in range(batch):
        hre_ref[b:b + 1, :] = fin[2 * b]
        him_ref[b:b + 1, :] = fin[2 * b + 1]

    yacc[...] += lax.dot_general(hprev[...].astype(BF16), cpow[...], _NT,
                                 preferred_element_type=F32)
    dsk = d_ref[...]
    for b in range(batch):
        for t in range(S5_CHUNK):
            rows = pl.ds(t, nch, stride=S5_CHUNK)
            y = (yacc[b * nch:(b + 1) * nch, t * HEAD_DIM:(t + 1) * HEAD_DIM]
                 + dsk * u_ref[b, rows, :])
            ys[b, rows, :] = _gelu_tanh(y)
    y_ref[...] = ys[...].astype(y_ref.dtype)


def _s5_tile_params(a_re, a_im, log_dt, b_re, b_im, c_re, c_im, d_skip):
    g, p = a_re.shape
    nt = g // S5_TILE_GROUPS
    lre = a_re.astype(F32).reshape(nt, 1, S5_TILE_STATE)
    lim = a_im.astype(F32).reshape(nt, 1, S5_TILE_STATE)
    ldt = jnp.repeat(log_dt.astype(F32), p).reshape(nt, 1, S5_TILE_STATE)

    def rows_gc(x):
        x = x.astype(F32).reshape(nt, HEAD_DIM, p)
        return jnp.tile(x, (1, 1, S5_TILE_GROUPS))

    btr = rows_gc(jnp.swapaxes(b_re, 1, 2))
    bti = rows_gc(jnp.swapaxes(b_im, 1, 2))
    cr = rows_gc(c_re)
    ci = rows_gc(c_im)
    dd = d_skip.astype(F32).reshape(1, g * S5_GROUP)
    return lre, lim, ldt, btr, bti, cr, ci, dd


def _s5_param_specs(n_lead):
    lead = (lambda i: (i, 0, 0))
    row = pl.BlockSpec((None, 1, S5_TILE_STATE), lead)
    mat = pl.BlockSpec((None, HEAD_DIM, S5_TILE_STATE), lead)
    del n_lead
    return [row, row, row, mat, mat, mat, mat, pl.BlockSpec((1, HEAD_DIM), lambda i: (0, i))]


def s5_prompt(u, tile_params):
    b, t, d = u.shape
    nt = d // HEAD_DIM
    nch = t // S5_CHUNK
    assert t % S5_CHUNK == 0 and nch % 8 == 0
    wide = S5_CHUNK * HEAD_DIM
    ns2 = 2 * S5_TILE_STATE
    return pl.pallas_call(
        functools.partial(_s5_prompt_kernel, batch=b, seq=t),
        out_shape=(jax.ShapeDtypeStruct((b, t, d), BF16),
                   jax.ShapeDtypeStruct((b, nt * S5_TILE_STATE), F32),
                   jax.ShapeDtypeStruct((b, nt * S5_TILE_STATE), F32)),
        grid=(nt,),
        in_specs=[pl.BlockSpec((b, t, HEAD_DIM), lambda i: (0, 0, i))] + _s5_param_specs(1),
        out_specs=(pl.BlockSpec((b, t, HEAD_DIM), lambda i: (0, 0, i)),
                   pl.BlockSpec((b, S5_TILE_STATE), lambda i: (0, i)),
                   pl.BlockSpec((b, S5_TILE_STATE), lambda i: (0, i))),
        scratch_shapes=[pltpu.VMEM((wide, wide), BF16),
                        pltpu.VMEM((wide, ns2), BF16),
                        pltpu.VMEM((wide, ns2), BF16),
                        pltpu.VMEM((b * nch, wide), BF16),
                        pltpu.VMEM((b * nch, ns2), F32),
                        pltpu.VMEM((b * nch, ns2), F32),
                        pltpu.VMEM((b * nch, wide), F32),
                        pltpu.VMEM((b, t, HEAD_DIM), F32)],
        compiler_params=_params(("parallel",), 58 << 20),
        name="s5_prompt",
    )(u, *tile_params)


def _s5_sample_kernel(u_ref, h0r_ref, h0i_ref, lre_ref, lim_ref, ldt_ref, btr_ref, bti_ref,
                      cr_ref, ci_ref, d_ref, y_ref, hre_ref, him_ref, *, tq):
    ab_re, ab_im, f_re, f_im = _s5_discretize(lre_ref[...], lim_ref[...], ldt_ref[...])
    bb_re, bb_im = _s5_bbar(btr_ref[...], bti_ref[...], f_re, f_im)
    m = _s5_tile_mask()
    c_cat = jnp.concatenate([jnp.where(m, cr_ref[...], 0.0), -jnp.where(m, ci_ref[...], 0.0)],
                            axis=1).astype(BF16)
    b_cat = jnp.concatenate([bb_re, bb_im], axis=1).astype(BF16)
    ns = S5_TILE_STATE
    h_re = h0r_ref[...]
    h_im = h0i_ref[...]
    dsk = d_ref[...]
    for t in range(tq):
        u = u_ref[t]
        bu = jnp.dot(u.astype(BF16), b_cat, preferred_element_type=F32)
        h_re, h_im = (ab_re * h_re - ab_im * h_im + bu[:, 0:ns],
                      ab_re * h_im + ab_im * h_re + bu[:, ns:2 * ns])
        h_cat = jnp.concatenate([h_re, h_im], axis=1).astype(BF16)
        y = lax.dot_general(h_cat, c_cat, _NT, preferred_element_type=F32) + dsk * u
        y_ref[t] = _gelu_tanh(y)
    hre_ref[...] = h_re
    him_ref[...] = h_im


def s5_sample(u_tb, h0_re, h0_im, tile_params):
    tq, b, d = u_tb.shape
    nt = d // HEAD_DIM
    st = pl.BlockSpec((b, S5_TILE_STATE), lambda i: (0, i))
    ublk = pl.BlockSpec((tq, b, HEAD_DIM), lambda i: (0, 0, i))
    return pl.pallas_call(
        functools.partial(_s5_sample_kernel, tq=tq),
        out_shape=(jax.ShapeDtypeStruct((tq, b, d), F32),
                   jax.ShapeDtypeStruct((b, nt * S5_TILE_STATE), F32),
                   jax.ShapeDtypeStruct((b, nt * S5_TILE_STATE), F32)),
        grid=(nt,),
        in_specs=[ublk, st, st] + _s5_param_specs(1),
        out_specs=(ublk, st, st),
        compiler_params=_params(("parallel",)),
        name="s5_sample",
    )(u_tb, h0_re, h0_im, *tile_params)


def _band_prompt_kernel(*refs, seq, groups):
    ng = len(groups)
    qkv = refs[:3 * ng]
    o_ref = refs[3 * ng]
    og, lg = refs[3 * ng + 1], refs[3 * ng + 2]
    kb = KEY_BLOCK
    a1 = lax.broadcasted_iota(jnp.int32, (kb, kb), 0)
    c1 = lax.broadcasted_iota(jnp.int32, (kb, kb), 1)
    a2 = lax.broadcasted_iota(jnp.int32, (kb, 2 * kb), 0)
    c2 = lax.broadcasted_iota(jnp.int32, (kb, 2 * kb), 1)
    first_valid = c1 <= a1
    band_valid = (c2 >= a2) & (c2 <= a2 + kb)

    for g, (win, dil) in enumerate(groups):
        assert win // dil == kb and seq % (dil * kb) == 0
        q_ref, k_ref, v_ref = qkv[3 * g], qkv[3 * g + 1], qkv[3 * g + 2]
        nb = seq // (dil * kb)

        def rows(r, n):
            if dil == 1:
                return pl.ds(n * kb, kb)
            return pl.ds(r + n * kb * dil, kb, stride=dil)

        for r in range(dil):
            for n in range(nb):
                qb = q_ref[rows(r, n), :].astype(BF16)
                if n == 0:
                    kk = k_ref[rows(r, 0), :].astype(BF16)
                    vv = v_ref[rows(r, 0), :].astype(BF16)
                    valid = first_valid
                else:
                    kk = jnp.concatenate([k_ref[rows(r, n - 1), :], k_ref[rows(r, n), :]],
                                         axis=0).astype(BF16)
                    vv = jnp.concatenate([v_ref[rows(r, n - 1), :], v_ref[rows(r, n), :]],
                                         axis=0).astype(BF16)
                    valid = band_valid
                s = lax.dot_general(qb, kk, _NT, preferred_element_type=F32) * ATTN_SCALE
                s = jnp.where(valid, s, -jnp.inf)
                mx = jnp.max(s, axis=-1, keepdims=True)
                p = jnp.exp(s - mx)
                l = jnp.sum(p, axis=-1, keepdims=True)
                o = jnp.dot(p.astype(BF16), vv, preferred_element_type=F32) / l
                og[g, rows(r, n), :] = o
                lg[g, rows(r, n), :] = jnp.broadcast_to(mx + jnp.log(l), (kb, HEAD_DIM))

    lses = [lg[g] for g in range(ng)]
    top = functools.reduce(jnp.maximum, lses)
    es = [jnp.exp(x - top) for x in lses]
    num = functools.reduce(lambda a, b: a + b, [e * og[g] for g, e in enumerate(es)])
    den = functools.reduce(lambda a, b: a + b, es)
    o_ref[...] = (num / den).astype(o_ref.dtype)


def band_prompt_attention(qkv, batch, seq, heads, groups):
    ng = len(groups)
    d = heads * HEAD_DIM

    def spec(c):
        return pl.BlockSpec((seq, HEAD_DIM), lambda b, h: (b, c * heads + h))

    return pl.pallas_call(
        functools.partial(_band_prompt_kernel, seq=seq, groups=groups),
        out_shape=jax.ShapeDtypeStruct((batch * seq, d), BF16),
        grid=(batch, heads),
        in_specs=[spec(c) for c in range(3 * ng)],
        out_specs=pl.BlockSpec((seq, HEAD_DIM), lambda b, h: (b, h)),
        scratch_shapes=[pltpu.VMEM((ng, seq, HEAD_DIM), F32),
                        pltpu.VMEM((ng, seq, HEAD_DIM), F32)],
        compiler_params=_params(("parallel", "parallel"), 48 << 20),
        name="band_prompt",
    )(*([qkv] * (3 * ng)))


def _band_sample_kernel(qkv_ref, kb_ref, vb_ref, o_ref, lse_ref, qbd, kn, vn, acc, mx_ref, l_ref,
                        *, heads, tq, group, dil, n_buf):
    c = pl.program_id(1)
    d = heads * HEAD_DIM
    rows = tq * heads
    tok = lax.broadcasted_iota(jnp.int32, (rows, KEY_BLOCK), 0) // heads
    lane = lax.broadcasted_iota(jnp.int32, (rows, KEY_BLOCK), 1)
    if dil == 1:
        new_valid = lane <= tok
        buf_valid = lane >= tok
    else:
        cls = min(dil, 8)
        new_valid = lane == tok
        buf_valid = (lane % cls) == tok

    def process(kb, vb, valid):
        s = lax.dot_general(qbd[...], kb, _NT, preferred_element_type=F32) * ATTN_SCALE
        s = jnp.where(valid, s, -jnp.inf)
        m_old = mx_ref[...]
        m_new = jnp.maximum(m_old, jnp.max(s, axis=-1, keepdims=True))
        p = jnp.exp(s - m_new)
        alpha = jnp.exp(m_old - m_new)
        l_ref[...] = alpha * l_ref[...] + jnp.sum(p, axis=-1, keepdims=True)
        mx_ref[...] = m_new
        acc[...] = alpha[:, 0:1] * acc[...] + jnp.dot(p.astype(BF16), vb,
                                                      preferred_element_type=F32)

    @pl.when(c == 0)
    def _():
        base = group * 3 * d
        _fill_block_diag_q(qbd, qkv_ref[:, base:base + d], heads, tq)
        kn[...] = jnp.zeros_like(kn)
        vn[...] = jnp.zeros_like(vn)
        kn[0:tq, :] = qkv_ref[:, base + d:base + 2 * d]
        vn[0:tq, :] = qkv_ref[:, base + 2 * d:base + 3 * d]
        acc[...] = jnp.zeros_like(acc)
        mx_ref[...] = jnp.full_like(mx_ref, NEG_BIG)
        l_ref[...] = jnp.zeros_like(l_ref)
        process(kn[...].astype(BF16), vn[...].astype(BF16), new_valid)

    @pl.when(c > 0)
    def _():
        process(kb_ref[...].reshape(KEY_BLOCK, d).astype(BF16),
                vb_ref[...].reshape(KEY_BLOCK, d).astype(BF16), buf_valid)

    @pl.when(c == n_buf)
    def _():
        l = l_ref[...]
        o = acc[...] / l[:, 0:1]
        lse = jnp.broadcast_to((mx_ref[...] + jnp.log(l))[:, 0:1], (rows, d))
        for t, r in enumerate(_take_head_diag(o, heads, tq)):
            o_ref[t:t + 1, :] = r
        for t, r in enumerate(_take_head_diag(lse, heads, tq)):
            lse_ref[t:t + 1, :] = r


def band_sample_attention(qkv3, buf_k, buf_v, layer, group, win, dil):
    b, tq, dall = qkv3.shape
    nc, _, wb, heads, hd = buf_k.shape
    d = heads * hd
    assert wb == win and win // dil == KEY_BLOCK and tq <= KEY_BLOCK
    if dil == 1:
        kv = buf_k.reshape(nc, b, wb, d), buf_v.reshape(nc, b, wb, d)
        n_buf = 1
        last = wb // KEY_BLOCK - 1
        blk = pl.BlockSpec((None, None, KEY_BLOCK, d), lambda bb, c: (layer, bb, last, 0))
    elif dil < 8:
        assert tq <= dil and KEY_BLOCK % dil == 0
        kv = buf_k.reshape(nc, b, wb, d), buf_v.reshape(nc, b, wb, d)
        n_buf = wb // KEY_BLOCK
        blk = pl.BlockSpec((None, None, KEY_BLOCK, d),
                           lambda bb, c: (layer, bb, jnp.maximum(c - 1, 0), 0))
    else:
        assert tq <= 8 and dil % 8 == 0
        kv = buf_k.reshape(nc, b, wb // dil, dil, d), buf_v.reshape(nc, b, wb // dil, dil, d)
        per = KEY_BLOCK // 8
        n_buf = (wb // dil) // per
        blk = pl.BlockSpec((None, None, per, 8, d),
                           lambda bb, c: (layer, bb, jnp.maximum(c - 1, 0), 0, 0))
    rows = tq * heads
    out = pl.BlockSpec((None, tq, d), lambda bb, c: (bb, 0, 0))
    return pl.pallas_call(
        functools.partial(_band_sample_kernel, heads=heads, tq=tq, group=group, dil=dil,
                          n_buf=n_buf),
        out_shape=(jax.ShapeDtypeStruct((b, tq, d), F32), jax.ShapeDtypeStruct((b, tq, d), F32)),
        grid=(b, n_buf + 1),
        in_specs=[pl.BlockSpec((None, tq, dall), lambda bb, c: (bb, 0, 0)), blk, blk],
        out_specs=(out, out),
        scratch_shapes=[pltpu.VMEM((rows, d), BF16),
                        pltpu.VMEM((KEY_BLOCK, d), F32),
                        pltpu.VMEM((KEY_BLOCK, d), F32),
                        pltpu.VMEM((rows, d), F32),
                        pltpu.VMEM((rows, KEY_BLOCK), F32),
                        pltpu.VMEM((rows, KEY_BLOCK), F32)],
        compiler_params=_params(("parallel", "arbitrary"), 40 << 20),
        name="band_sample",
    )(qkv3, *kv)


def _merge_groups_kernel(*refs, ng):
    outs, lses, o_ref = refs[:ng], refs[ng:2 * ng], refs[2 * ng]
    ls = [r[...] for r in lses]
    top = functools.reduce(jnp.maximum, ls)
    es = [jnp.exp(x - top) for x in ls]
    num = functools.reduce(lambda a, b: a + b, [e * r[...] for e, r in zip(es, outs)])
    den = functools.reduce(lambda a, b: a + b, es)
    o_ref[...] = (num / den).astype(o_ref.dtype)


def merge_groups(outs, lses):
    m, d = outs[0].shape
    return pl.pallas_call(
        functools.partial(_merge_groups_kernel, ng=len(outs)),
        out_shape=jax.ShapeDtypeStruct((m, d), BF16),
        name="merge_groups",
    )(*outs, *lses)


def _ffn(x, layer, norm_ffn, ffn_w1, ffn_w2):
    hn = rmsnorm(x, norm_ffn, layer)
    hid = matmul(hn, [ffn_w1], layer, ffn_w1.shape[-1], epi="relu2", out_dtype=BF16)
    return matmul_ktiled(hid, ffn_w2, layer, x)


def _qkv_gain(q_gain, k_gain, d):
    reps = d // HEAD_DIM
    return jnp.concatenate([jnp.tile(q_gain.astype(F32), reps), jnp.tile(k_gain.astype(F32), reps),
                            jnp.ones((d,), F32)])


def kernel(x_prompt, x_sample, cache_a_k, cache_a_v, state_b_re, state_b_im, cache_c_k_w128, cache_c_v_w128, cache_c_k_w512, cache_c_v_w512, cache_c_k_w2048, cache_c_v_w2048, page_table, norm_mix, norm_ffn, a_w_qkv, a_q_gain, a_k_gain, a_logit_bias, a_w_o, b_w_in, b_a_re, b_a_im, b_log_dt, b_b_re, b_b_im, b_c_re, b_c_im, b_d, b_w_glu, b_w_gate, c_w_qkv, c_q_gain, c_k_gain, c_w_o, ffn_w1, ffn_w2):
    bp, tp, d = x_prompt.shape
    bs, ts, _ = x_sample.shape
    heads = d // HEAD_DIM
    depth = norm_mix.shape[0]
    c_buf_k = (cache_c_k_w128, cache_c_k_w512, cache_c_k_w2048)
    c_buf_v = (cache_c_v_w128, cache_c_v_w512, cache_c_v_w2048)
    ng = len(C_GROUPS)
    xp = x_prompt.reshape(bp * tp, d)
    xs = x_sample.reshape(bs * ts, d)
    ak_p, av_p, ak_s, av_s = [], [], [], []
    br_p, bi_p, br_s, bi_s = [], [], [], []
    ck_p = [[] for _ in C_GROUPS]
    cv_p = [[] for _ in C_GROUPS]
    ck_s = [[] for _ in C_GROUPS]
    cv_s = [[] for _ in C_GROUPS]
    ia = ib = ic = 0
    for layer in range(depth):
        hp = rmsnorm(xp, norm_mix, layer)
        hs = rmsnorm(xs, norm_mix, layer)
        kind = layer % N_MIXERS
        if kind == 0:
            gain = _qkv_gain(a_q_gain[ia], a_k_gain[ia], d).reshape(1, 3 * d)
            qkv_p = matmul(hp, [a_w_qkv], ia, 3 * d, epi="headnorm", gain=gain)
            qkv_s = matmul(hs, [a_w_qkv], ia, 3 * d, epi="headnorm", gain=gain)
            op = sb_prompt_attention(qkv_p, a_logit_bias[ia].astype(F32), bp, tp, heads)
            os_ = sb_sample_attention(qkv_s.reshape(bs, ts, 3 * d), cache_a_k, cache_a_v, ia,
                                      page_table, a_logit_bias[ia])
            xp = matmul(op, [a_w_o], ia, d, res=xp)
            xs = matmul(os_.reshape(bs * ts, d).astype(BF16), [a_w_o], ia, d, res=xs)
            ak_p.append(qkv_p[:, d:2 * d].reshape(bp, tp, heads, HEAD_DIM))
            av_p.append(qkv_p[:, 2 * d:].reshape(bp, tp, heads, HEAD_DIM))
            ak_s.append(qkv_s[:, d:2 * d].reshape(bs, ts, heads, HEAD_DIM))
            av_s.append(qkv_s[:, 2 * d:].reshape(bs, ts, heads, HEAD_DIM))
            ia += 1
        elif kind == 1:
            tile_params = _s5_tile_params(b_a_re[ib], b_a_im[ib], b_log_dt[ib], b_b_re[ib],
                                          b_b_im[ib], b_c_re[ib], b_c_im[ib], b_d[ib])
            up = matmul(hp, [b_w_in], ib, d)
            us = matmul(hs, [b_w_in], ib, d)
            yp, rp, ip = s5_prompt(up.reshape(bp, tp, d), tile_params)
            us_tb = jnp.swapaxes(us.reshape(bs, ts, d), 0, 1)
            ys_tb, rn, im_ = s5_sample(us_tb, state_b_re[ib].reshape(bs, -1),
                                       state_b_im[ib].reshape(bs, -1), tile_params)
            ys = jnp.swapaxes(ys_tb, 0, 1).reshape(bs * ts, d).astype(BF16)
            xp = matmul(yp.reshape(bp * tp, d), [b_w_glu, b_w_gate], ib, d, epi="glu", res=xp,
                        bn=256)
            xs = matmul(ys, [b_w_glu, b_w_gate], ib, d, epi="glu", res=xs, bn=256)
            g = d // S5_GROUP
            br_p.append(rp.reshape(bp, g, S5_STATE))
            bi_p.append(ip.reshape(bp, g, S5_STATE))
            br_s.append(rn.reshape(bs, g, S5_STATE))
            bi_s.append(im_.reshape(bs, g, S5_STATE))
            ib += 1
        else:
            gain = jnp.concatenate([_qkv_gain(c_q_gain[ic, gi], c_k_gain[ic, gi], d)
                                    for gi in range(ng)]).reshape(1, ng * 3 * d)
            qkv_p = matmul(hp, [c_w_qkv], ic, ng * 3 * d, epi="headnorm", gain=gain)
            qkv_s = matmul(hs, [c_w_qkv], ic, ng * 3 * d, epi="headnorm", gain=gain)
            op = band_prompt_attention(qkv_p, bp, tp, heads, C_GROUPS)
            qkv_s3 = qkv_s.reshape(bs, ts, ng * 3 * d)
            outs, lses = [], []
            for gi, (win, dil) in enumerate(C_GROUPS):
                o_g, l_g = band_sample_attention(qkv_s3, c_buf_k[gi], c_buf_v[gi], ic, gi, win, dil)
                outs.append(o_g.reshape(bs * ts, d))
                lses.append(l_g.reshape(bs * ts, d))
            os_ = merge_groups(outs, lses)
            xp = matmul(op, [c_w_o], ic, d, res=xp)
            xs = matmul(os_, [c_w_o], ic, d, res=xs)
            qp5 = qkv_p.reshape(bp, tp, ng, 3, heads, HEAD_DIM)
            qs5 = qkv_s.reshape(bs, ts, ng, 3, heads, HEAD_DIM)
            for gi, (win, dil) in enumerate(C_GROUPS):
                keep = min(win, tp)
                ck_p[gi].append(qp5[:, tp - keep:, gi, 1])
                cv_p[gi].append(qp5[:, tp - keep:, gi, 2])
                wb = c_buf_k[gi].shape[2]
                ck_s[gi].append(jnp.concatenate([c_buf_k[gi][ic], qs5[:, :, gi, 1]], axis=1)[:, -wb:])
                cv_s[gi].append(jnp.concatenate([c_buf_v[gi][ic], qs5[:, :, gi, 2]], axis=1)[:, -wb:])
            ic += 1
        xp = _ffn(xp, layer, norm_ffn, ffn_w1, ffn_w2)
        xs = _ffn(xs, layer, norm_ffn, ffn_w1, ffn_w2)
    return (xp.reshape(bp, tp, d), xs.reshape(bs, ts, d),
            jnp.stack(ak_p), jnp.stack(av_p), jnp.stack(ak_s), jnp.stack(av_s),
            jnp.stack(br_p), jnp.stack(bi_p), jnp.stack(br_s), jnp.stack(bi_s),
            jnp.stack(ck_p[0]), jnp.stack(cv_p[0]), jnp.stack(ck_p[1]), jnp.stack(cv_p[1]),
            jnp.stack(ck_p[2]), jnp.stack(cv_p[2]),
            jnp.stack(ck_s[0]), jnp.stack(cv_s[0]), jnp.stack(ck_s[1]), jnp.stack(cv_s[1]),
            jnp.stack(ck_s[2]), jnp.stack(cv_s[2]))
```

```python
import functools
import math

import jax
import jax.numpy as jnp
from jax import lax
from jax.experimental import pallas as pl
from jax.experimental.pallas import tpu as pltpu

F32 = jnp.float32
BF16 = jnp.bfloat16

HEAD_DIM = 128
KEY_BLOCK = 128
EPS = 1e-6
ATTN_SCALE = HEAD_DIM ** -0.5
N_MIXERS = 3
S5_GROUP = 16
S5_STATE = 64
S5_TILE_GROUPS = HEAD_DIM // S5_GROUP
S5_TILE_STATE = S5_TILE_GROUPS * S5_STATE
S5_CHUNK = 16
C_GROUPS = ((128, 1), (512, 4), (2048, 16))
VMEM_CAP = 60 * 1024 * 1024
NEG_BIG = -1e30

_NT = (((1,), (1,)), ((), ()))


def _params(sem, vmem_bytes=None):
    limit = None if vmem_bytes is None else int(min(VMEM_CAP, vmem_bytes))
    return pltpu.CompilerParams(dimension_semantics=sem, vmem_limit_bytes=limit)


def _rmsnorm_kernel(x_ref, g_ref, o_ref):
    x = x_ref[...]
    ms = jnp.mean(x * x, axis=-1, keepdims=True)
    o_ref[...] = (x * lax.rsqrt(ms + EPS) * g_ref[...]).astype(o_ref.dtype)


def rmsnorm(x, gains, layer):
    m, d = x.shape
    bm = min(m, 256)
    g3 = gains.reshape(gains.shape[0], 1, d)
    return pl.pallas_call(
        _rmsnorm_kernel,
        out_shape=jax.ShapeDtypeStruct((m, d), BF16),
        grid=(m // bm,),
        in_specs=[pl.BlockSpec((bm, d), lambda i: (i, 0)),
                  pl.BlockSpec((None, 1, d), lambda i: (layer, 0, 0))],
        out_specs=pl.BlockSpec((bm, d), lambda i: (i, 0)),
        compiler_params=_params(("parallel",)),
        name="rmsnorm",
    )(x, g3)


def _mm_tile(x, wbf, epi, sec, gain_ref, res_ref, o_ref):
    acc = jnp.dot(x, wbf[0][...], preferred_element_type=F32)
    if epi == "glu":
        acc = acc * jax.nn.sigmoid(jnp.dot(x, wbf[1][...], preferred_element_type=F32))
    elif epi == "relu2":
        r = jnp.maximum(acc, 0.0)
        acc = r * r
    if epi == "headnorm":
        @pl.when(sec < 2)
        def _():
            for c in range(acc.shape[1] // HEAD_DIM):
                sl = slice(c * HEAD_DIM, (c + 1) * HEAD_DIM)
                y = acc[:, sl]
                ms = jnp.mean(y * y, axis=-1, keepdims=True)
                o_ref[:, sl] = (y * lax.rsqrt(ms + EPS) * gain_ref[:, sl]).astype(o_ref.dtype)

        @pl.when(sec == 2)
        def _():
            o_ref[...] = acc.astype(o_ref.dtype)
    else:
        if res_ref is not None:
            acc = res_ref[...] + acc
        o_ref[...] = acc.astype(o_ref.dtype)


def _mm_kernel(*refs, n_w, epi, has_res, sec_blocks):
    it = iter(refs)
    x_ref, xs_ref = next(it), next(it)
    w_refs = [next(it) for _ in range(n_w)]
    gain_ref = next(it) if epi == "headnorm" else None
    res_ref, ress_ref = (next(it), next(it)) if has_res else (None, None)
    o_ref, os_ref = next(it), next(it)
    wbf = [next(it) for _ in range(n_w)]
    sec = (pl.program_id(0) // sec_blocks) % 3

    @pl.when(pl.program_id(1) == 0)
    def _():
        for w_ref, s in zip(w_refs, wbf):
            s[...] = w_ref[...].astype(BF16)
        _mm_tile(xs_ref[...], wbf, epi, sec, gain_ref, ress_ref, os_ref)

    _mm_tile(x_ref[...], wbf, epi, sec, gain_ref, res_ref, o_ref)


def matmul(x, xs, ws, layer, n_out, *, epi="none", res=None, res_s=None, gain=None,
           out_dtype=F32, bn=512):
    m, k = x.shape
    ms = xs.shape[0]
    n_w = len(ws)
    bm = min(m, 1024)
    bn = math.gcd(bn, n_out, k if epi == "headnorm" else n_out)
    assert m % bm == 0 and bn % HEAD_DIM == 0
    grid = (n_out // bn, m // bm)
    in_specs = [pl.BlockSpec((bm, k), lambda j, i: (i, 0)),
                pl.BlockSpec((ms, k), lambda j, i: (0, 0))]
    in_specs += [pl.BlockSpec((None, k, bn), lambda j, i: (layer, 0, j)) for _ in ws]
    args = [x, xs, *ws]
    sec_blocks = 1
    if epi == "headnorm":
        assert k % bn == 0
        sec_blocks = k // bn
        in_specs.append(pl.BlockSpec((1, bn), lambda j, i: (0, j)))
        args.append(gain)
    if res is not None:
        in_specs += [pl.BlockSpec((bm, bn), lambda j, i: (i, j)),
                     pl.BlockSpec((ms, bn), lambda j, i: (0, j))]
        args += [res, res_s]
    out_bytes = jnp.dtype(out_dtype).itemsize
    vmem = (2 * (bm + ms) * k * 2 + n_w * (2 * k * bn * 4 + k * bn * 2)
            + 2 * (bm + ms) * bn * (out_bytes + (4 if res is not None else 0))
            + (2 + n_w) * bm * bn * 4 + (4 << 20))
    return pl.pallas_call(
        functools.partial(_mm_kernel, n_w=n_w, epi=epi, has_res=res is not None,
                          sec_blocks=sec_blocks),
        out_shape=(jax.ShapeDtypeStruct((m, n_out), out_dtype),
                   jax.ShapeDtypeStruct((ms, n_out), out_dtype)),
        grid=grid,
        in_specs=in_specs,
        out_specs=(pl.BlockSpec((bm, bn), lambda j, i: (i, j)),
                   pl.BlockSpec((ms, bn), lambda j, i: (0, j))),
        scratch_shapes=[pltpu.VMEM((k, bn), BF16) for _ in ws],
        compiler_params=_params(("parallel", "arbitrary"), vmem),
        name="mm_" + epi,
    )(*args)


def _mm_kt_kernel(x_ref, xs_ref, w_ref, res_ref, ress_ref, o_ref, os_ref, acc_ref, accs_ref, *, nk):
    i = pl.program_id(1)
    kk = pl.program_id(2)
    w = w_ref[...].astype(BF16)

    @pl.when(kk == 0)
    def _():
        acc_ref[...] = jnp.zeros_like(acc_ref)

    acc_ref[...] += jnp.dot(x_ref[...], w, preferred_element_type=F32)

    @pl.when(kk == nk - 1)
    def _():
        o_ref[...] = res_ref[...] + acc_ref[...]

    @pl.when(i == 0)
    def _():
        @pl.when(kk == 0)
        def _():
            accs_ref[...] = jnp.zeros_like(accs_ref)

        accs_ref[...] += jnp.dot(xs_ref[...], w, preferred_element_type=F32)

        @pl.when(kk == nk - 1)
        def _():
            os_ref[...] = ress_ref[...] + accs_ref[...]


def matmul_ktiled(x, xs, w, layer, res, res_s, *, bn=1024, bk=2048):
    m, k = x.shape
    ms = xs.shape[0]
    n = w.shape[-1]
    bm = min(m, 1024)
    bn = min(bn, n)
    bk = min(bk, k)
    assert m % bm == 0 and n % bn == 0 and k % bk == 0
    nk = k // bk
    vmem = (2 * (bm + ms) * bk * 2 + 2 * bk * bn * 4 + bk * bn * 2 + 5 * (bm + ms) * bn * 4
            + (4 << 20))
    return pl.pallas_call(
        functools.partial(_mm_kt_kernel, nk=nk),
        out_shape=(jax.ShapeDtypeStruct((m, n), F32), jax.ShapeDtypeStruct((ms, n), F32)),
        grid=(n // bn, m // bm, nk),
        in_specs=[pl.BlockSpec((bm, bk), lambda j, i, kk: (i, kk)),
                  pl.BlockSpec((ms, bk), lambda j, i, kk: (0, kk)),
                  pl.BlockSpec((None, bk, bn), lambda j, i, kk: (layer, kk, j)),
                  pl.BlockSpec((bm, bn), lambda j, i, kk: (i, j)),
                  pl.BlockSpec((ms, bn), lambda j, i, kk: (0, j))],
        out_specs=(pl.BlockSpec((bm, bn), lambda j, i, kk: (i, j)),
                   pl.BlockSpec((ms, bn), lambda j, i, kk: (0, j))),
        scratch_shapes=[pltpu.VMEM((bm, bn), F32), pltpu.VMEM((ms, bn), F32)],
        compiler_params=_params(("parallel", "arbitrary", "arbitrary"), vmem),
        name="mm_ktiled",
    )(x, xs, w, res, res_s)


def _cumsum_matrix():
    jj = lax.broadcasted_iota(jnp.int32, (KEY_BLOCK, 2 * KEY_BLOCK), 0)
    ss = lax.broadcasted_iota(jnp.int32, (KEY_BLOCK, 2 * KEY_BLOCK), 1)
    return jnp.where((ss >= KEY_BLOCK) | (jj > ss), 1.0, 0.0).astype(BF16)


def _log_keep(z):
    return -(jnp.maximum(z, 0.0) + jnp.log(1.0 + jnp.exp(-jnp.abs(z))))


def _sb_prompt_kernel(bias_ref, q_ref, k_ref, v_ref, o_ref, *, bq):
    h = pl.program_id(1)
    qi = pl.program_id(2)
    nsub = bq // KEY_BLOCK
    bias = bias_ref[h]
    q = q_ref[...].astype(BF16)
    cs = _cumsum_matrix()
    row = lax.broadcasted_iota(jnp.int32, (bq, KEY_BLOCK), 0)
    col = lax.broadcasted_iota(jnp.int32, (bq, KEY_BLOCK), 1)

    def group(j, run, acc, diagonal):
        ks = pl.multiple_of(j * bq, bq)
        kk = k_ref[pl.ds(ks, bq), :].astype(BF16)
        vv = v_ref[pl.ds(ks, bq), :].astype(BF16)
        z = lax.dot_general(q, kk, _NT, preferred_element_type=F32) * ATTN_SCALE + bias
        parts = []
        for g in range(nsub):
            zg = z[:, g * KEY_BLOCK:(g + 1) * KEY_BLOCK]
            valid = (col + g * KEY_BLOCK) < row if diagonal else None
            lk = _log_keep(zg)
            if diagonal:
                lk = jnp.where(valid, lk, 0.0)
            hi = lk.astype(BF16)
            lo = (lk - hi.astype(F32)).astype(BF16)
            sums = (jnp.dot(hi, cs, preferred_element_type=F32)
                    + jnp.dot(lo, cs, preferred_element_type=F32))
            parts.append((zg + lk + sums[:, :KEY_BLOCK], sums[:, KEY_BLOCK:], valid))
        ws = [None] * nsub
        for g in reversed(range(nsub)):
            logw, rowsum, valid = parts[g]
            w = jnp.exp(logw + run)
            if diagonal:
                w = jnp.where(valid, w, 0.0)
            ws[g] = w.astype(BF16)
            run = run + rowsum
        acc = acc + jnp.dot(jnp.concatenate(ws, axis=1), vv, preferred_element_type=F32)
        return run, acc

    run = jnp.zeros((bq, KEY_BLOCK), F32)
    acc = jnp.zeros((bq, HEAD_DIM), F32)
    run, acc = group(qi, run, acc, True)

    def body(it, carry):
        return group(qi - 1 - it, carry[0], carry[1], False)

    run, acc = lax.fori_loop(0, qi, body, (run, acc))
    o_ref[...] = acc.astype(o_ref.dtype)


def sb_prompt_attention(qkv, bias, batch, seq, heads):
    bq = min(seq, 512)
    nq = seq // bq
    d = heads * HEAD_DIM
    return pl.pallas_call(
        functools.partial(_sb_prompt_kernel, bq=bq),
        out_shape=jax.ShapeDtypeStruct((batch * seq, d), BF16),
        grid=(batch, heads, nq),
        in_specs=[pl.BlockSpec(memory_space=pltpu.SMEM),
                  pl.BlockSpec((bq, HEAD_DIM), lambda b, h, i: (b * nq + i, h)),
                  pl.BlockSpec((seq, HEAD_DIM), lambda b, h, i: (b, heads + h)),
                  pl.BlockSpec((seq, HEAD_DIM), lambda b, h, i: (b, 2 * heads + h))],
        out_specs=pl.BlockSpec((bq, HEAD_DIM), lambda b, h, i: (b * nq + i, h)),
        compiler_params=_params(("parallel", "parallel", "arbitrary")),
        name="sb_prompt",
    )(bias, qkv, qkv, qkv)


def _head_match(heads, rows):
    return (lax.broadcasted_iota(jnp.int32, (heads, rows), 0)
            == lax.broadcasted_iota(jnp.int32, (heads, rows), 1) % heads)


def _fill_queries(qd, q, heads, tq):
    for t in range(tq):
        for h in range(heads):
            r = t * heads + h
            qd[r:r + 1, :] = q[t:t + 1, h * HEAD_DIM:(h + 1) * HEAD_DIM]


def _fill_new_rows(dst, src, heads, tq):
    dst[...] = jnp.zeros_like(dst)
    for t in range(tq):
        for h in range(heads):
            dst[t, h:h + 1, :] = src[t:t + 1, h * HEAD_DIM:(h + 1) * HEAD_DIM]


def _key_logits(k3, qd, heads, z_ref):
    rows = qd.shape[0]
    k2 = k3.reshape(KEY_BLOCK * heads, HEAD_DIM).astype(BF16)
    zt = lax.dot_general(k2, qd, _NT, preferred_element_type=F32).reshape(KEY_BLOCK, heads, rows)
    z_ref[...] = jnp.sum(jnp.where(_head_match(heads, rows)[None], zt, 0.0), axis=1)
    return z_ref[...]


def _weighted_values(w, v3, heads):
    rows = w.shape[1]
    wexp = jnp.where(_head_match(heads, rows)[None], w[:, None, :], 0.0)
    wexp = wexp.reshape(KEY_BLOCK * heads, rows).astype(BF16)
    v2 = v3.reshape(KEY_BLOCK * heads, HEAD_DIM).astype(BF16)
    return lax.dot_general(wexp, v2, (((0,), (0,)), ((), ())), preferred_element_type=F32)


def _lanes_to_rows(x):
    return jnp.transpose(jnp.broadcast_to(x, (HEAD_DIM, x.shape[1])))


def _store_by_head(o_ref, a, heads, tq):
    for t in range(tq):
        for h in range(heads):
            r = t * heads + h
            o_ref[t:t + 1, h * HEAD_DIM:(h + 1) * HEAD_DIM] = a[r:r + 1, :].astype(o_ref.dtype)


def _sb_sample_kernel(pt_ref, bias_ref, qkv_ref, kc_ref, vc_ref, o_ref,
                      qd, kn, vn, acc, run_ref, z_ref, *, heads, tq, n_pages):
    del pt_ref
    p = pl.program_id(1)
    d = heads * HEAD_DIM
    rows = tq * heads
    key = lax.broadcasted_iota(jnp.int32, (KEY_BLOCK, KEY_BLOCK), 0)
    other = lax.broadcasted_iota(jnp.int32, (KEY_BLOCK, KEY_BLOCK), 1)
    later = jnp.where(other > key, 1.0, 0.0).astype(BF16)

    def process(k3, v3, valid):
        z = _key_logits(k3, qd[...].astype(BF16), heads, z_ref) * ATTN_SCALE + bias_ref[...]
        lk = _log_keep(z)
        if valid is not None:
            lk = jnp.where(valid, lk, 0.0)
        hi = lk.astype(BF16)
        lo = (lk - hi.astype(F32)).astype(BF16)
        sums = (jnp.dot(later, hi, preferred_element_type=F32)
                + jnp.dot(later, lo, preferred_element_type=F32))
        w = jnp.exp(z + lk + sums + run_ref[...])
        if valid is not None:
            w = jnp.where(valid, w, 0.0)
        run_ref[...] += jnp.sum(lk, axis=0, keepdims=True)
        acc[...] += _weighted_values(w, v3, heads)

    @pl.when(p == 0)
    def _():
        _fill_queries(qd, qkv_ref[:, 0:d], heads, tq)
        _fill_new_rows(kn, qkv_ref[:, d:2 * d], heads, tq)
        _fill_new_rows(vn, qkv_ref[:, 2 * d:3 * d], heads, tq)
        acc[...] = jnp.zeros_like(acc)
        run_ref[...] = jnp.zeros_like(run_ref)
        s_idx = lax.broadcasted_iota(jnp.int32, (KEY_BLOCK, rows), 0)
        t_idx = lax.broadcasted_iota(jnp.int32, (KEY_BLOCK, rows), 1) // heads
        process(kn[...], vn[...], s_idx < t_idx)

    @pl.when(p > 0)
    def _():
        process(kc_ref[...], vc_ref[...], None)

    @pl.when(p == n_pages)
    def _():
        _store_by_head(o_ref, acc[...], heads, tq)


def sb_sample_attention(qkv3, cache_k, cache_v, layer, page_table, bias):
    b, tq, d3 = qkv3.shape
    d = d3 // 3
    heads = d // HEAD_DIM
    rows = tq * heads
    n_pages = page_table.shape[1]
    page = cache_k.shape[2]
    assert page == KEY_BLOCK and tq <= KEY_BLOCK
    bias_rows = jnp.tile(bias.astype(F32), tq).reshape(1, rows)

    def page_map(bb, p, pt):
        return (layer, pt[bb, n_pages - jnp.maximum(p, 1)], 0, 0, 0)

    grid_spec = pltpu.PrefetchScalarGridSpec(
        num_scalar_prefetch=1,
        grid=(b, n_pages + 1),
        in_specs=[pl.BlockSpec((1, rows), lambda bb, p, pt: (0, 0)),
                  pl.BlockSpec((None, tq, d3), lambda bb, p, pt: (bb, 0, 0)),
                  pl.BlockSpec((None, None, page, heads, HEAD_DIM), page_map),
                  pl.BlockSpec((None, None, page, heads, HEAD_DIM), page_map)],
        out_specs=pl.BlockSpec((None, tq, d), lambda bb, p, pt: (bb, 0, 0)),
        scratch_shapes=[pltpu.VMEM((rows, HEAD_DIM), F32),
                        pltpu.VMEM((KEY_BLOCK, heads, HEAD_DIM), F32),
                        pltpu.VMEM((KEY_BLOCK, heads, HEAD_DIM), F32),
                        pltpu.VMEM((rows, HEAD_DIM), F32),
                        pltpu.VMEM((1, rows), F32),
                        pltpu.VMEM((KEY_BLOCK, rows), F32)])
    return pl.pallas_call(
        functools.partial(_sb_sample_kernel, heads=heads, tq=tq, n_pages=n_pages),
        out_shape=jax.ShapeDtypeStruct((b, tq, d), F32),
        grid_spec=grid_spec,
        compiler_params=_params(("parallel", "arbitrary"), 40 << 20),
        name="sb_sample",
    )(page_table, bias_rows, qkv3, cache_k, cache_v)


def _s5_discretize(lam_re, lam_im, log_dt):
    dt = jnp.exp(log_dt)
    mag = jnp.exp(lam_re * dt)
    ab_re = mag * jnp.cos(lam_im * dt)
    ab_im = mag * jnp.sin(lam_im * dt)
    den = lam_re * lam_re + lam_im * lam_im
    nr = ab_re - 1.0
    ni = ab_im
    f_re = (nr * lam_re + ni * lam_im) / den
    f_im = (ni * lam_re - nr * lam_im) / den
    return ab_re, ab_im, f_re, f_im


def _s5_tile_mask():
    r = lax.broadcasted_iota(jnp.int32, (HEAD_DIM, S5_TILE_STATE), 0) // S5_GROUP
    c = lax.broadcasted_iota(jnp.int32, (HEAD_DIM, S5_TILE_STATE), 1) // S5_STATE
    return r == c


def _s5_bbar(bt_re, bt_im, f_re, f_im):
    m = _s5_tile_mask()
    bb_re = jnp.where(m, f_re * bt_re - f_im * bt_im, 0.0)
    bb_im = jnp.where(m, f_re * bt_im + f_im * bt_re, 0.0)
    return bb_re, bb_im


def _gelu_tanh(x):
    return 0.5 * x * (1.0 + jnp.tanh(0.7978845608028654 * (x + 0.044715 * (x * x * x))))


def _s5_prompt_kernel(u_ref, lre_ref, lim_ref, ldt_ref, btr_ref, bti_ref, cr_ref, ci_ref, d_ref,
                      y_ref, hre_ref, him_ref,
                      toep, bend, cpow, ucat, sloc, hprev, yacc, ys, *, batch, seq):
    nch = seq // S5_CHUNK
    ns = S5_TILE_STATE
    ab_re, ab_im, f_re, f_im = _s5_discretize(lre_ref[...], lim_ref[...], ldt_ref[...])
    bb_re, bb_im = _s5_bbar(btr_ref[...], bti_ref[...], f_re, f_im)
    m = _s5_tile_mask()
    c_re = jnp.where(m, cr_ref[...], 0.0)
    c_im = jnp.where(m, ci_ref[...], 0.0)
    c_cat = jnp.concatenate([c_re, -c_im], axis=1).astype(BF16)

    toep[...] = jnp.zeros_like(toep)
    p_re = jnp.ones_like(ab_re)
    p_im = jnp.zeros_like(ab_im)
    for tau in range(S5_CHUNK):
        bt_cat = jnp.concatenate([bb_re * p_re - bb_im * p_im, bb_re * p_im + bb_im * p_re],
                                 axis=1).astype(BF16)
        s_blk = S5_CHUNK - 1 - tau
        bend[s_blk * HEAD_DIM:(s_blk + 1) * HEAD_DIM, :] = bt_cat
        k_tau = lax.dot_general(bt_cat, c_cat, _NT, preferred_element_type=F32).astype(BF16)
        for s in range(S5_CHUNK - tau):
            t = s + tau
            toep[s * HEAD_DIM:(s + 1) * HEAD_DIM, t * HEAD_DIM:(t + 1) * HEAD_DIM] = k_tau
        p_re, p_im = p_re * ab_re - p_im * ab_im, p_re * ab_im + p_im * ab_re
        cpow[tau * HEAD_DIM:(tau + 1) * HEAD_DIM, :] = jnp.concatenate(
            [c_re * p_re - c_im * p_im, -(c_re * p_im + c_im * p_re)], axis=1).astype(BF16)
    a16_re, a16_im = p_re, p_im

    for b in range(batch):
        for s in range(S5_CHUNK):
            ucat[b * nch:(b + 1) * nch, s * HEAD_DIM:(s + 1) * HEAD_DIM] = (
                u_ref[b, pl.ds(s, nch, stride=S5_CHUNK), :].astype(BF16))
    uc = ucat[...]
    yacc[...] = jnp.dot(uc, toep[...], preferred_element_type=F32)
    sloc[...] = jnp.dot(uc, bend[...], preferred_element_type=F32)

    def step(n, carry):
        out = []
        for b in range(batch):
            h_re, h_im = carry[2 * b], carry[2 * b + 1]
            r = b * nch + n
            hprev[pl.ds(r, 1), 0:ns] = h_re
            hprev[pl.ds(r, 1), ns:2 * ns] = h_im
            s_re = sloc[pl.ds(r, 1), 0:ns]
            s_im = sloc[pl.ds(r, 1), ns:2 * ns]
            out.append(a16_re * h_re - a16_im * h_im + s_re)
            out.append(a16_re * h_im + a16_im * h_re + s_im)
        return tuple(out)

    zero = jnp.zeros((1, ns), F32)
    fin = lax.fori_loop(0, nch, step, (zero,) * (2 * batch))
    for b in range(batch):
        hre_ref[b:b + 1, :] = fin[2 * b]
        him_ref[b:b + 1, :] = fin[2 * b + 1]

    yacc[...] += lax.dot_general(hprev[...].astype(BF16), cpow[...], _NT,
                                 preferred_element_type=F32)
    dsk = d_ref[...]
    for b in range(batch):
        for t in range(S5_CHUNK):
            rows = pl.ds(t, nch, stride=S5_CHUNK)
            y = (yacc[b * nch:(b + 1) * nch, t * HEAD_DIM:(t + 1) * HEAD_DIM]
                 + dsk * u_ref[b, rows, :])
            ys[b, rows, :] = _gelu_tanh(y)
    y_ref[...] = ys[...].astype(y_ref.dtype)


def _s5_tile_params(a_re, a_im, log_dt, b_re, b_im, c_re, c_im, d_skip):
    g, p = a_re.shape
    nt = g // S5_TILE_GROUPS
    lre = a_re.astype(F32).reshape(nt, 1, S5_TILE_STATE)
    lim = a_im.astype(F32).reshape(nt, 1, S5_TILE_STATE)
    ldt = jnp.repeat(log_dt.astype(F32), p).reshape(nt, 1, S5_TILE_STATE)

    def rows_gc(x):
        x = x.astype(F32).reshape(nt, HEAD_DIM, p)
        return jnp.tile(x, (1, 1, S5_TILE_GROUPS))

    btr = rows_gc(jnp.swapaxes(b_re, 1, 2))
    bti = rows_gc(jnp.swapaxes(b_im, 1, 2))
    cr = rows_gc(c_re)
    ci = rows_gc(c_im)
    dd = d_skip.astype(F32).reshape(1, g * S5_GROUP)
    return lre, lim, ldt, btr, bti, cr, ci, dd


def _s5_param_specs(n_lead):
    lead = (lambda i: (i, 0, 0))
    row = pl.BlockSpec((None, 1, S5_TILE_STATE), lead)
    mat = pl.BlockSpec((None, HEAD_DIM, S5_TILE_STATE), lead)
    del n_lead
    return [row, row, row, mat, mat, mat, mat, pl.BlockSpec((1, HEAD_DIM), lambda i: (0, i))]


def s5_prompt(u, tile_params):
    b, t, d = u.shape
    nt = d // HEAD_DIM
    nch = t // S5_CHUNK
    assert t % S5_CHUNK == 0 and nch % 8 == 0
    wide = S5_CHUNK * HEAD_DIM
    ns2 = 2 * S5_TILE_STATE
    return pl.pallas_call(
        functools.partial(_s5_prompt_kernel, batch=b, seq=t),
        out_shape=(jax.ShapeDtypeStruct((b, t, d), BF16),
                   jax.ShapeDtypeStruct((b, nt * S5_TILE_STATE), F32),
                   jax.ShapeDtypeStruct((b, nt * S5_TILE_STATE), F32)),
        grid=(nt,),
        in_specs=[pl.BlockSpec((b, t, HEAD_DIM), lambda i: (0, 0, i))] + _s5_param_specs(1),
        out_specs=(pl.BlockSpec((b, t, HEAD_DIM), lambda i: (0, 0, i)),
                   pl.BlockSpec((b, S5_TILE_STATE), lambda i: (0, i)),
                   pl.BlockSpec((b, S5_TILE_STATE), lambda i: (0, i))),
        scratch_shapes=[pltpu.VMEM((wide, wide), BF16),
                        pltpu.VMEM((wide, ns2), BF16),
                        pltpu.VMEM((wide, ns2), BF16),
                        pltpu.VMEM((b * nch, wide), BF16),
                        pltpu.VMEM((b * nch, ns2), F32),
                        pltpu.VMEM((b * nch, ns2), F32),
                        pltpu.VMEM((b * nch, wide), F32),
                        pltpu.VMEM((b, t, HEAD_DIM), F32)],
        compiler_params=_params(("parallel",), 58 << 20),
        name="s5_prompt",
    )(u, *tile_params)


def _s5_sample_kernel(u_ref, h0r_ref, h0i_ref, lre_ref, lim_ref, ldt_ref, btr_ref, bti_ref,
                      cr_ref, ci_ref, d_ref, y_ref, hre_ref, him_ref, *, tq):
    ab_re, ab_im, f_re, f_im = _s5_discretize(lre_ref[...], lim_ref[...], ldt_ref[...])
    bb_re, bb_im = _s5_bbar(btr_ref[...], bti_ref[...], f_re, f_im)
    m = _s5_tile_mask()
    c_cat = jnp.concatenate([jnp.where(m, cr_ref[...], 0.0), -jnp.where(m, ci_ref[...], 0.0)],
                            axis=1).astype(BF16)
    b_cat = jnp.concatenate([bb_re, bb_im], axis=1).astype(BF16)
    ns = S5_TILE_STATE
    h_re = h0r_ref[...]
    h_im = h0i_ref[...]
    dsk = d_ref[...]
    for t in range(tq):
        u = u_ref[t]
        bu = jnp.dot(u.astype(BF16), b_cat, preferred_element_type=F32)
        h_re, h_im = (ab_re * h_re - ab_im * h_im + bu[:, 0:ns],
                      ab_re * h_im + ab_im * h_re + bu[:, ns:2 * ns])
        h_cat = jnp.concatenate([h_re, h_im], axis=1).astype(BF16)
        y = lax.dot_general(h_cat, c_cat, _NT, preferred_element_type=F32) + dsk * u
        y_ref[t] = _gelu_tanh(y)
    hre_ref[...] = h_re
    him_ref[...] = h_im


def s5_sample(u_tb, h0_re, h0_im, tile_params):
    tq, b, d = u_tb.shape
    nt = d // HEAD_DIM
    st = pl.BlockSpec((b, S5_TILE_STATE), lambda i: (0, i))
    ublk = pl.BlockSpec((tq, b, HEAD_DIM), lambda i: (0, 0, i))
    return pl.pallas_call(
        functools.partial(_s5_sample_kernel, tq=tq),
        out_shape=(jax.ShapeDtypeStruct((tq, b, d), F32),
                   jax.ShapeDtypeStruct((b, nt * S5_TILE_STATE), F32),
                   jax.ShapeDtypeStruct((b, nt * S5_TILE_STATE), F32)),
        grid=(nt,),
        in_specs=[ublk, st, st] + _s5_param_specs(1),
        out_specs=(ublk, st, st),
        compiler_params=_params(("parallel",)),
        name="s5_sample",
    )(u_tb, h0_re, h0_im, *tile_params)


def _band_prompt_kernel(*refs, seq, groups):
    ng = len(groups)
    qkv = refs[:3 * ng]
    o_ref = refs[3 * ng]
    og, lg = refs[3 * ng + 1], refs[3 * ng + 2]
    kb = KEY_BLOCK
    a1 = lax.broadcasted_iota(jnp.int32, (kb, kb), 0)
    c1 = lax.broadcasted_iota(jnp.int32, (kb, kb), 1)
    a2 = lax.broadcasted_iota(jnp.int32, (kb, 2 * kb), 0)
    c2 = lax.broadcasted_iota(jnp.int32, (kb, 2 * kb), 1)
    first_valid = c1 <= a1
    band_valid = (c2 >= a2) & (c2 <= a2 + kb)

    for g, (win, dil) in enumerate(groups):
        assert win // dil == kb and seq % (dil * kb) == 0
        q_ref, k_ref, v_ref = qkv[3 * g], qkv[3 * g + 1], qkv[3 * g + 2]
        nb = seq // (dil * kb)

        def rows(r, n):
            if dil == 1:
                return pl.ds(n * kb, kb)
            return pl.ds(r + n * kb * dil, kb, stride=dil)

        for r in range(dil):
            for n in range(nb):
                qb = q_ref[rows(r, n), :].astype(BF16)
                if n == 0:
                    kk = k_ref[rows(r, 0), :].astype(BF16)
                    vv = v_ref[rows(r, 0), :].astype(BF16)
                    valid = first_valid
                else:
                    kk = jnp.concatenate([k_ref[rows(r, n - 1), :], k_ref[rows(r, n), :]],
                                         axis=0).astype(BF16)
                    vv = jnp.concatenate([v_ref[rows(r, n - 1), :], v_ref[rows(r, n), :]],
                                         axis=0).astype(BF16)
                    valid = band_valid
                s = lax.dot_general(qb, kk, _NT, preferred_element_type=F32) * ATTN_SCALE
                s = jnp.where(valid, s, -jnp.inf)
                mx = jnp.max(s, axis=-1, keepdims=True)
                p = jnp.exp(s - mx)
                l = jnp.sum(p, axis=-1, keepdims=True)
                o = jnp.dot(p.astype(BF16), vv, preferred_element_type=F32) / l
                og[g, rows(r, n), :] = o
                lg[g, rows(r, n), :] = jnp.broadcast_to(mx + jnp.log(l), (kb, HEAD_DIM))

    lses = [lg[g] for g in range(ng)]
    top = functools.reduce(jnp.maximum, lses)
    es = [jnp.exp(x - top) for x in lses]
    num = functools.reduce(lambda a, b: a + b, [e * og[g] for g, e in enumerate(es)])
    den = functools.reduce(lambda a, b: a + b, es)
    o_ref[...] = (num / den).astype(o_ref.dtype)


def band_prompt_attention(qkv, batch, seq, heads, groups):
    ng = len(groups)
    d = heads * HEAD_DIM

    def spec(c):
        return pl.BlockSpec((seq, HEAD_DIM), lambda b, h: (b, c * heads + h))

    return pl.pallas_call(
        functools.partial(_band_prompt_kernel, seq=seq, groups=groups),
        out_shape=jax.ShapeDtypeStruct((batch * seq, d), BF16),
        grid=(batch, heads),
        in_specs=[spec(c) for c in range(3 * ng)],
        out_specs=pl.BlockSpec((seq, HEAD_DIM), lambda b, h: (b, h)),
        scratch_shapes=[pltpu.VMEM((ng, seq, HEAD_DIM), F32),
                        pltpu.VMEM((ng, seq, HEAD_DIM), F32)],
        compiler_params=_params(("parallel", "parallel"), 48 << 20),
        name="band_prompt",
    )(*([qkv] * (3 * ng)))


def _band_sample_kernel(qkv_ref, kb_ref, vb_ref, o_ref, lse_ref, qd, kn, vn, acc, mx_ref, l_ref,
                        z_ref, *, heads, tq, group, dil, n_buf):
    c = pl.program_id(1)
    d = heads * HEAD_DIM
    rows = tq * heads
    key = lax.broadcasted_iota(jnp.int32, (KEY_BLOCK, rows), 0)
    tok = lax.broadcasted_iota(jnp.int32, (KEY_BLOCK, rows), 1) // heads
    if dil == 1:
        new_valid = key <= tok
        buf_valid = key >= tok
    else:
        new_valid = key == tok
        buf_valid = tok == c - 1

    def process(k3, v3, valid):
        s = _key_logits(k3, qd[...].astype(BF16), heads, z_ref) * ATTN_SCALE
        s = jnp.where(valid, s, -jnp.inf)
        m_old = mx_ref[...]
        m_new = jnp.maximum(m_old, jnp.max(s, axis=0, keepdims=True))
        p = jnp.exp(s - m_new)
        alpha = jnp.exp(m_old - m_new)
        l_ref[...] = alpha * l_ref[...] + jnp.sum(p, axis=0, keepdims=True)
        mx_ref[...] = m_new
        acc[...] = _lanes_to_rows(alpha) * acc[...] + _weighted_values(p, v3, heads)

    @pl.when(c == 0)
    def _():
        base = group * 3 * d
        _fill_queries(qd, qkv_ref[:, base:base + d], heads, tq)
        _fill_new_rows(kn, qkv_ref[:, base + d:base + 2 * d], heads, tq)
        _fill_new_rows(vn, qkv_ref[:, base + 2 * d:base + 3 * d], heads, tq)
        acc[...] = jnp.zeros_like(acc)
        mx_ref[...] = jnp.full_like(mx_ref, NEG_BIG)
        l_ref[...] = jnp.zeros_like(l_ref)
        process(kn[...], vn[...], new_valid)

    @pl.when(c > 0)
    def _():
        process(kb_ref[...], vb_ref[...], buf_valid)

    @pl.when(c == n_buf)
    def _():
        l = l_ref[...]
        _store_by_head(o_ref, acc[...] / _lanes_to_rows(l), heads, tq)
        _store_by_head(lse_ref, _lanes_to_rows(mx_ref[...] + jnp.log(l)), heads, tq)


def band_sample_attention(qkv3, buf_k, buf_v, layer, group, win, dil):
    b, tq, dall = qkv3.shape
    nc, _, wb, heads, hd = buf_k.shape
    d = heads * hd
    assert wb == win and win // dil == KEY_BLOCK and tq <= KEY_BLOCK
    if dil == 1:
        kv = buf_k, buf_v
        n_buf = 1
        last = wb // KEY_BLOCK - 1
        blk = pl.BlockSpec((None, None, KEY_BLOCK, heads, hd),
                           lambda bb, c: (layer, bb, last, 0, 0))
    else:
        assert tq <= dil
        shape = (nc, b, wb // dil, dil, heads, hd)
        kv = buf_k.reshape(shape), buf_v.reshape(shape)
        n_buf = tq
        blk = pl.BlockSpec((None, None, KEY_BLOCK, None, heads, hd),
                           lambda bb, c: (layer, bb, 0, jnp.maximum(c - 1, 0), 0, 0))
    rows = tq * heads
    out = pl.BlockSpec((None, tq, d), lambda bb, c: (bb, 0, 0))
    return pl.pallas_call(
        functools.partial(_band_sample_kernel, heads=heads, tq=tq, group=group, dil=dil,
                          n_buf=n_buf),
        out_shape=(jax.ShapeDtypeStruct((b, tq, d), F32), jax.ShapeDtypeStruct((b, tq, d), F32)),
        grid=(b, n_buf + 1),
        in_specs=[pl.BlockSpec((None, tq, dall), lambda bb, c: (bb, 0, 0)), blk, blk],
        out_specs=(out, out),
        scratch_shapes=[pltpu.VMEM((rows, HEAD_DIM), F32),
                        pltpu.VMEM((KEY_BLOCK, heads, HEAD_DIM), F32),
                        pltpu.VMEM((KEY_BLOCK, heads, HEAD_DIM), F32),
                        pltpu.VMEM((rows, HEAD_DIM), F32),
                        pltpu.VMEM((1, rows), F32),
                        pltpu.VMEM((1, rows), F32),
                        pltpu.VMEM((KEY_BLOCK, rows), F32)],
        compiler_params=_params(("parallel", "arbitrary"), 40 << 20),
        name="band_sample",
    )(qkv3, *kv)


def _merge_groups_kernel(*refs, ng):
    outs, lses, o_ref = refs[:ng], refs[ng:2 * ng], refs[2 * ng]
    ls = [r[...] for r in lses]
    top = functools.reduce(jnp.maximum, ls)
    es = [jnp.exp(x - top) for x in ls]
    num = functools.reduce(lambda a, b: a + b, [e * r[...] for e, r in zip(es, outs)])
    den = functools.reduce(lambda a, b: a + b, es)
    o_ref[...] = (num / den).astype(o_ref.dtype)


def merge_groups(outs, lses):
    m, d = outs[0].shape
    return pl.pallas_call(
        functools.partial(_merge_groups_kernel, ng=len(outs)),
        out_shape=jax.ShapeDtypeStruct((m, d), BF16),
        name="merge_groups",
    )(*outs, *lses)


def _ffn(xp, xs, layer, norm_ffn, ffn_w1, ffn_w2):
    hp = rmsnorm(xp, norm_ffn, layer)
    hs = rmsnorm(xs, norm_ffn, layer)
    hid_p, hid_s = matmul(hp, hs, [ffn_w1], layer, ffn_w1.shape[-1], epi="relu2", out_dtype=BF16)
    return matmul_ktiled(hid_p, hid_s, ffn_w2, layer, xp, xs)


def _qkv_gain(q_gain, k_gain, d):
    reps = d // HEAD_DIM
    return jnp.concatenate([jnp.tile(q_gain.astype(F32), reps), jnp.tile(k_gain.astype(F32), reps),
                            jnp.ones((d,), F32)])


def kernel(x_prompt, x_sample, cache_a_k, cache_a_v, state_b_re, state_b_im, cache_c_k_w128, cache_c_v_w128, cache_c_k_w512, cache_c_v_w512, cache_c_k_w2048, cache_c_v_w2048, page_table, norm_mix, norm_ffn, a_w_qkv, a_q_gain, a_k_gain, a_logit_bias, a_w_o, b_w_in, b_a_re, b_a_im, b_log_dt, b_b_re, b_b_im, b_c_re, b_c_im, b_d, b_w_glu, b_w_gate, c_w_qkv, c_q_gain, c_k_gain, c_w_o, ffn_w1, ffn_w2):
    bp, tp, d = x_prompt.shape
    bs, ts, _ = x_sample.shape
    heads = d // HEAD_DIM
    depth = norm_mix.shape[0]
    c_buf_k = (cache_c_k_w128, cache_c_k_w512, cache_c_k_w2048)
    c_buf_v = (cache_c_v_w128, cache_c_v_w512, cache_c_v_w2048)
    ng = len(C_GROUPS)
    xp = x_prompt.reshape(bp * tp, d)
    xs = x_sample.reshape(bs * ts, d)
    ak_p, av_p, ak_s, av_s = [], [], [], []
    br_p, bi_p, br_s, bi_s = [], [], [], []
    ck_p = [[] for _ in C_GROUPS]
    cv_p = [[] for _ in C_GROUPS]
    ck_s = [[] for _ in C_GROUPS]
    cv_s = [[] for _ in C_GROUPS]
    ia = ib = ic = 0
    for layer in range(depth):
        hp = rmsnorm(xp, norm_mix, layer)
        hs = rmsnorm(xs, norm_mix, layer)
        kind = layer % N_MIXERS
        if kind == 0:
            gain = _qkv_gain(a_q_gain[ia], a_k_gain[ia], d).reshape(1, 3 * d)
            qkv_p, qkv_s = matmul(hp, hs, [a_w_qkv], ia, 3 * d, epi="headnorm", gain=gain)
            op = sb_prompt_attention(qkv_p, a_logit_bias[ia].astype(F32), bp, tp, heads)
            os_ = sb_sample_attention(qkv_s.reshape(bs, ts, 3 * d), cache_a_k, cache_a_v, ia,
                                      page_table, a_logit_bias[ia])
            xp, xs = matmul(op, os_.reshape(bs * ts, d).astype(BF16), [a_w_o], ia, d,
                            res=xp, res_s=xs)
            ak_p.append(qkv_p[:, d:2 * d].reshape(bp, tp, heads, HEAD_DIM))
            av_p.append(qkv_p[:, 2 * d:].reshape(bp, tp, heads, HEAD_DIM))
            ak_s.append(qkv_s[:, d:2 * d].reshape(bs, ts, heads, HEAD_DIM))
            av_s.append(qkv_s[:, 2 * d:].reshape(bs, ts, heads, HEAD_DIM))
            ia += 1
        elif kind == 1:
            tile_params = _s5_tile_params(b_a_re[ib], b_a_im[ib], b_log_dt[ib], b_b_re[ib],
                                          b_b_im[ib], b_c_re[ib], b_c_im[ib], b_d[ib])
            up, us = matmul(hp, hs, [b_w_in], ib, d)
            yp, rp, ip = s5_prompt(up.reshape(bp, tp, d), tile_params)
            us_tb = jnp.swapaxes(us.reshape(bs, ts, d), 0, 1)
            ys_tb, rn, im_ = s5_sample(us_tb, state_b_re[ib].reshape(bs, -1),
                                       state_b_im[ib].reshape(bs, -1), tile_params)
            ys = jnp.swapaxes(ys_tb, 0, 1).reshape(bs * ts, d).astype(BF16)
            xp, xs = matmul(yp.reshape(bp * tp, d), ys, [b_w_glu, b_w_gate], ib, d, epi="glu",
                            res=xp, res_s=xs, bn=256)
            g = d // S5_GROUP
            br_p.append(rp.reshape(bp, g, S5_STATE))
            bi_p.append(ip.reshape(bp, g, S5_STATE))
            br_s.append(rn.reshape(bs, g, S5_STATE))
            bi_s.append(im_.reshape(bs, g, S5_STATE))
            ib += 1
        else:
            gain = jnp.concatenate([_qkv_gain(c_q_gain[ic, gi], c_k_gain[ic, gi], d)
                                    for gi in range(ng)]).reshape(1, ng * 3 * d)
            qkv_p, qkv_s = matmul(hp, hs, [c_w_qkv], ic, ng * 3 * d, epi="headnorm", gain=gain)
            op = band_prompt_attention(qkv_p, bp, tp, heads, C_GROUPS)
            qkv_s3 = qkv_s.reshape(bs, ts, ng * 3 * d)
            outs, lses = [], []
            for gi, (win, dil) in enumerate(C_GROUPS):
                o_g, l_g = band_sample_attention(qkv_s3, c_buf_k[gi], c_buf_v[gi], ic, gi, win, dil)
                outs.append(o_g.reshape(bs * ts, d))
                lses.append(l_g.reshape(bs * ts, d))
            os_ = merge_groups(outs, lses)
            xp, xs = matmul(op, os_, [c_w_o], ic, d, res=xp, res_s=xs)
            qp5 = qkv_p.reshape(bp, tp, ng, 3, heads, HEAD_DIM)
            qs5 = qkv_s.reshape(bs, ts, ng, 3, heads, HEAD_DIM)
            for gi, (win, dil) in enumerate(C_GROUPS):
                keep = min(win, tp)
                ck_p[gi].append(qp5[:, tp - keep:, gi, 1])
                cv_p[gi].append(qp5[:, tp - keep:, gi, 2])
                wb = c_buf_k[gi].shape[2]
                ck_s[gi].append(jnp.concatenate([c_buf_k[gi][ic], qs5[:, :, gi, 1]], axis=1)[:, -wb:])
                cv_s[gi].append(jnp.concatenate([c_buf_v[gi][ic], qs5[:, :, gi, 2]], axis=1)[:, -wb:])
            ic += 1
        xp, xs = _ffn(xp, xs, layer, norm_ffn, ffn_w1, ffn_w2)
    return (xp.reshape(bp, tp, d), xs.reshape(bs, ts, d),
            jnp.stack(ak_p), jnp.stack(av_p), jnp.stack(ak_s), jnp.stack(av_s),
            jnp.stack(br_p), jnp.stack(bi_p), jnp.stack(br_s), jnp.stack(bi_s),
            jnp.stack(ck_p[0]), jnp.stack(cv_p[0]), jnp.stack(ck_p[1]), jnp.stack(cv_p[1]),
            jnp.stack(ck_p[2]), jnp.stack(cv_p[2]),
            jnp.stack(ck_s[0]), jnp.stack(cv_s[0]), jnp.stack(ck_s[1]), jnp.stack(cv_s[1]),
            jnp.stack(ck_s[2]), jnp.stack(cv_s[2]))
```

```python
import functools
import math

import jax
import jax.numpy as jnp
from jax import lax
from jax.experimental import pallas as pl
from jax.experimental.pallas import tpu as pltpu

F32 = jnp.float32
BF16 = jnp.bfloat16

HEAD_DIM = 128
KEY_BLOCK = 128
EPS = 1e-6
ATTN_SCALE = HEAD_DIM ** -0.5
N_MIXERS = 3
S5_GROUP = 16
S5_STATE = 64
S5_TILE_GROUPS = HEAD_DIM // S5_GROUP
S5_TILE_STATE = S5_TILE_GROUPS * S5_STATE
S5_CHUNK = 16
C_GROUPS = ((128, 1), (512, 4), (2048, 16))
VMEM_CAP = 60 * 1024 * 1024
NEG_BIG = -1e30

_NT = (((1,), (1,)), ((), ()))


def _params(sem, vmem_bytes=None):
    limit = None if vmem_bytes is None else int(min(VMEM_CAP, vmem_bytes))
    return pltpu.CompilerParams(dimension_semantics=sem, vmem_limit_bytes=limit)


def _rmsnorm_kernel(x_ref, g_ref, o_ref):
    x = x_ref[...]
    ms = jnp.mean(x * x, axis=-1, keepdims=True)
    o_ref[...] = (x * lax.rsqrt(ms + EPS) * g_ref[...]).astype(o_ref.dtype)


def rmsnorm(x, gains, layer):
    m, d = x.shape
    bm = min(m, 256)
    g3 = gains.reshape(gains.shape[0], 1, d)
    return pl.pallas_call(
        _rmsnorm_kernel,
        out_shape=jax.ShapeDtypeStruct((m, d), BF16),
        grid=(m // bm,),
        in_specs=[pl.BlockSpec((bm, d), lambda i: (i, 0)),
                  pl.BlockSpec((None, 1, d), lambda i: (layer, 0, 0))],
        out_specs=pl.BlockSpec((bm, d), lambda i: (i, 0)),
        compiler_params=_params(("parallel",)),
        name="rmsnorm",
    )(x, g3)


def _mm_tile(x, wbf, epi, sec, gain_ref, res_ref, o_ref):
    acc = jnp.dot(x, wbf[0][...], preferred_element_type=F32)
    if epi == "glu":
        acc = acc * jax.nn.sigmoid(jnp.dot(x, wbf[1][...], preferred_element_type=F32))
    elif epi == "relu2":
        r = jnp.maximum(acc, 0.0)
        acc = r * r
    if epi == "headnorm":
        @pl.when(sec < 2)
        def _():
            for c in range(acc.shape[1] // HEAD_DIM):
                sl = slice(c * HEAD_DIM, (c + 1) * HEAD_DIM)
                y = acc[:, sl]
                ms = jnp.mean(y * y, axis=-1, keepdims=True)
                o_ref[:, sl] = (y * lax.rsqrt(ms + EPS) * gain_ref[:, sl]).astype(o_ref.dtype)

        @pl.when(sec == 2)
        def _():
            o_ref[...] = acc.astype(o_ref.dtype)
    else:
        if res_ref is not None:
            acc = res_ref[...] + acc
        o_ref[...] = acc.astype(o_ref.dtype)


def _mm_kernel(*refs, n_w, epi, has_res, sec_blocks):
    it = iter(refs)
    x_ref, xs_ref = next(it), next(it)
    w_refs = [next(it) for _ in range(n_w)]
    gain_ref = next(it) if epi == "headnorm" else None
    res_ref, ress_ref = (next(it), next(it)) if has_res else (None, None)
    o_ref, os_ref = next(it), next(it)
    wbf = [next(it) for _ in range(n_w)]
    sec = (pl.program_id(0) // sec_blocks) % 3

    @pl.when(pl.program_id(1) == 0)
    def _():
        for w_ref, s in zip(w_refs, wbf):
            s[...] = w_ref[...].astype(BF16)
        _mm_tile(xs_ref[...], wbf, epi, sec, gain_ref, ress_ref, os_ref)

    _mm_tile(x_ref[...], wbf, epi, sec, gain_ref, res_ref, o_ref)


def matmul(x, xs, ws, layer, n_out, *, epi="none", res=None, res_s=None, gain=None,
           out_dtype=F32, bm=1024, bn=512):
    m, k = x.shape
    ms = xs.shape[0]
    n_w = len(ws)
    bm = min(m, bm)
    bn = math.gcd(bn, n_out, k if epi == "headnorm" else n_out)
    assert m % bm == 0 and bn % HEAD_DIM == 0
    grid = (n_out // bn, m // bm)
    in_specs = [pl.BlockSpec((bm, k), lambda j, i: (i, 0)),
                pl.BlockSpec((ms, k), lambda j, i: (0, 0))]
    in_specs += [pl.BlockSpec((None, k, bn), lambda j, i: (layer, 0, j)) for _ in ws]
    args = [x, xs, *ws]
    sec_blocks = 1
    if epi == "headnorm":
        assert k % bn == 0
        sec_blocks = k // bn
        in_specs.append(pl.BlockSpec((1, bn), lambda j, i: (0, j)))
        args.append(gain)
    if res is not None:
        in_specs += [pl.BlockSpec((bm, bn), lambda j, i: (i, j)),
                     pl.BlockSpec((ms, bn), lambda j, i: (0, j))]
        args += [res, res_s]
    out_bytes = jnp.dtype(out_dtype).itemsize
    vmem = (2 * (bm + ms) * k * 2 + n_w * (2 * k * bn * 4 + k * bn * 2)
            + 2 * (bm + ms) * bn * (out_bytes + (4 if res is not None else 0))
            + (2 + n_w) * bm * bn * 4 + (4 << 20))
    return pl.pallas_call(
        functools.partial(_mm_kernel, n_w=n_w, epi=epi, has_res=res is not None,
                          sec_blocks=sec_blocks),
        out_shape=(jax.ShapeDtypeStruct((m, n_out), out_dtype),
                   jax.ShapeDtypeStruct((ms, n_out), out_dtype)),
        grid=grid,
        in_specs=in_specs,
        out_specs=(pl.BlockSpec((bm, bn), lambda j, i: (i, j)),
                   pl.BlockSpec((ms, bn), lambda j, i: (0, j))),
        scratch_shapes=[pltpu.VMEM((k, bn), BF16) for _ in ws],
        compiler_params=_params(("parallel", "arbitrary"), vmem),
        name="mm_" + epi,
    )(*args)


def _mm_kt_kernel(x_ref, xs_ref, w_ref, res_ref, ress_ref, o_ref, os_ref):
    i = pl.program_id(1)
    kk = pl.program_id(2)
    w = w_ref[...].astype(BF16)

    @pl.when(kk == 0)
    def _():
        o_ref[...] = res_ref[...]

    o_ref[...] += jnp.dot(x_ref[...], w, preferred_element_type=F32)

    @pl.when(i == 0)
    def _():
        @pl.when(kk == 0)
        def _():
            os_ref[...] = ress_ref[...]

        os_ref[...] += jnp.dot(xs_ref[...], w, preferred_element_type=F32)


_WIDE_TILE = dict(bm=512, bn=1024)


def matmul_ktiled(x, xs, w, layer, res, res_s, *, bm=2048, bn=1024, bk=1024):
    m, k = x.shape
    ms = xs.shape[0]
    n = w.shape[-1]
    bm = min(m, bm)
    bn = min(bn, n)
    bk = min(bk, k)
    assert m % bm == 0 and n % bn == 0 and k % bk == 0
    vmem = (2 * (bm + ms) * bk * 2 + 2 * bk * bn * 4 + bk * bn * 2 + 5 * (bm + ms) * bn * 4
            + (4 << 20))
    return pl.pallas_call(
        _mm_kt_kernel,
        out_shape=(jax.ShapeDtypeStruct((m, n), F32), jax.ShapeDtypeStruct((ms, n), F32)),
        grid=(n // bn, m // bm, k // bk),
        in_specs=[pl.BlockSpec((bm, bk), lambda j, i, kk: (i, kk)),
                  pl.BlockSpec((ms, bk), lambda j, i, kk: (0, kk)),
                  pl.BlockSpec((None, bk, bn), lambda j, i, kk: (layer, kk, j)),
                  pl.BlockSpec((bm, bn), lambda j, i, kk: (i, j)),
                  pl.BlockSpec((ms, bn), lambda j, i, kk: (0, j))],
        out_specs=(pl.BlockSpec((bm, bn), lambda j, i, kk: (i, j)),
                   pl.BlockSpec((ms, bn), lambda j, i, kk: (0, j))),
        compiler_params=_params(("parallel", "arbitrary", "arbitrary"), vmem),
        name="mm_ktiled",
    )(x, xs, w, res, res_s)


def _log_keep(z):
    return -(jnp.maximum(z, 0.0) + jnp.log(1.0 + jnp.exp(-jnp.abs(z))))


def _sb_prompt_kernel(bias_ref, q_ref, k_ref, v_ref, o_ref, *, bq):
    h = pl.program_id(1)
    qi = pl.program_id(2)
    nsub = bq // KEY_BLOCK
    bias = bias_ref[h]
    q = (q_ref[...] * ATTN_SCALE).astype(BF16)
    key_j = lax.broadcasted_iota(jnp.int32, (KEY_BLOCK, 2 * KEY_BLOCK), 0)
    key_s = lax.broadcasted_iota(jnp.int32, (KEY_BLOCK, 2 * KEY_BLOCK), 1)
    cs = jnp.where((key_s >= KEY_BLOCK) | (key_j > key_s), 1.0, 0.0).astype(BF16)
    ahead = [lax.broadcasted_iota(jnp.int32, (bq, KEY_BLOCK), 1) + g * KEY_BLOCK
             - lax.broadcasted_iota(jnp.int32, (bq, KEY_BLOCK), 0) for g in range(nsub)]

    def group(j, run, acc, diagonal):
        ks = pl.multiple_of(j * bq, bq)
        kk = k_ref[pl.ds(ks, bq), :].astype(BF16)
        vv = v_ref[pl.ds(ks, bq), :].astype(BF16)
        z = lax.dot_general(q, kk, _NT, preferred_element_type=F32) + bias
        parts = []
        for g in range(nsub):
            zg = z[:, g * KEY_BLOCK:(g + 1) * KEY_BLOCK]
            lk = _log_keep(zg)
            if diagonal:
                lk = jnp.where(ahead[g] < 0, lk, 0.0)
            sums = jnp.dot(lk.astype(BF16), cs, preferred_element_type=F32)
            parts.append((zg + lk + sums[:, :KEY_BLOCK], sums[:, KEY_BLOCK:]))
        ws = [None] * nsub
        for g in reversed(range(nsub)):
            logw, total = parts[g]
            w = jnp.exp(logw + run)
            if diagonal:
                w = jnp.where(ahead[g] < 0, w, 0.0)
            ws[g] = w.astype(BF16)
            run = run + total
        acc = acc + jnp.dot(jnp.concatenate(ws, axis=1), vv, preferred_element_type=F32)
        return run, acc

    run = jnp.zeros((bq, KEY_BLOCK), F32)
    acc = jnp.zeros((bq, HEAD_DIM), F32)
    run, acc = group(qi, run, acc, True)

    def body(it, carry):
        return group(qi - 1 - it, carry[0], carry[1], False)

    run, acc = lax.fori_loop(0, qi, body, (run, acc))
    o_ref[...] = acc.astype(o_ref.dtype)


def sb_prompt_attention(qkv, bias, batch, seq, heads):
    bq = min(seq, 512)
    nq = seq // bq
    d = heads * HEAD_DIM
    return pl.pallas_call(
        functools.partial(_sb_prompt_kernel, bq=bq),
        out_shape=jax.ShapeDtypeStruct((batch * seq, d), BF16),
        grid=(batch, heads, nq),
        in_specs=[pl.BlockSpec(memory_space=pltpu.SMEM),
                  pl.BlockSpec((bq, HEAD_DIM), lambda b, h, i: (b * nq + i, h)),
                  pl.BlockSpec((seq, HEAD_DIM), lambda b, h, i: (b, heads + h)),
                  pl.BlockSpec((seq, HEAD_DIM), lambda b, h, i: (b, 2 * heads + h))],
        out_specs=pl.BlockSpec((bq, HEAD_DIM), lambda b, h, i: (b * nq + i, h)),
        compiler_params=_params(("parallel", "parallel", "arbitrary")),
        name="sb_prompt",
    )(bias, qkv, qkv, qkv)


def _head_match(heads, rows):
    return (lax.broadcasted_iota(jnp.int32, (heads, rows), 0)
            == lax.broadcasted_iota(jnp.int32, (heads, rows), 1) % heads)


def _fill_queries(qd, q, heads, tq):
    for t in range(tq):
        for h in range(heads):
            r = t * heads + h
            qd[r:r + 1, :] = q[t:t + 1, h * HEAD_DIM:(h + 1) * HEAD_DIM]


def _fill_new_rows(dst, src, heads, tq):
    dst[...] = jnp.zeros_like(dst)
    for t in range(tq):
        for h in range(heads):
            dst[t, h:h + 1, :] = src[t:t + 1, h * HEAD_DIM:(h + 1) * HEAD_DIM]


def _key_logits(k3, qd, heads, z_ref):
    rows = qd.shape[0]
    k2 = k3.reshape(KEY_BLOCK * heads, HEAD_DIM).astype(BF16)
    zt = lax.dot_general(k2, qd, _NT, preferred_element_type=F32).reshape(KEY_BLOCK, heads, rows)
    z_ref[...] = jnp.sum(jnp.where(_head_match(heads, rows)[None], zt, 0.0), axis=1)
    return z_ref[...]


def _weighted_values(w, v3, heads):
    rows = w.shape[1]
    wexp = jnp.where(_head_match(heads, rows)[None], w[:, None, :], 0.0)
    wexp = wexp.reshape(KEY_BLOCK * heads, rows).astype(BF16)
    v2 = v3.reshape(KEY_BLOCK * heads, HEAD_DIM).astype(BF16)
    return lax.dot_general(wexp, v2, (((0,), (0,)), ((), ())), preferred_element_type=F32)


def _lanes_to_rows(x):
    return jnp.transpose(jnp.broadcast_to(x, (HEAD_DIM, x.shape[1])))


def _store_by_head(o_ref, a, heads, tq):
    for t in range(tq):
        for h in range(heads):
            r = t * heads + h
            o_ref[t:t + 1, h * HEAD_DIM:(h + 1) * HEAD_DIM] = a[r:r + 1, :].astype(o_ref.dtype)


def _sb_sample_kernel(pt_ref, bias_ref, qkv_ref, kc_ref, vc_ref, o_ref,
                      qd, kn, vn, acc, run_ref, z_ref, *, heads, tq, n_pages):
    del pt_ref
    p = pl.program_id(1)
    d = heads * HEAD_DIM
    rows = tq * heads
    key = lax.broadcasted_iota(jnp.int32, (KEY_BLOCK, KEY_BLOCK), 0)
    other = lax.broadcasted_iota(jnp.int32, (KEY_BLOCK, KEY_BLOCK), 1)
    later = jnp.where(other > key, 1.0, 0.0).astype(BF16)

    def process(k3, v3, valid):
        z = _key_logits(k3, qd[...].astype(BF16), heads, z_ref) * ATTN_SCALE + bias_ref[...]
        lk = _log_keep(z)
        if valid is not None:
            lk = jnp.where(valid, lk, 0.0)
        hi = lk.astype(BF16)
        lo = (lk - hi.astype(F32)).astype(BF16)
        sums = (jnp.dot(later, hi, preferred_element_type=F32)
                + jnp.dot(later, lo, preferred_element_type=F32))
        w = jnp.exp(z + lk + sums + run_ref[...])
        if valid is not None:
            w = jnp.where(valid, w, 0.0)
        run_ref[...] += jnp.sum(lk, axis=0, keepdims=True)
        acc[...] += _weighted_values(w, v3, heads)

    @pl.when(p == 0)
    def _():
        _fill_queries(qd, qkv_ref[:, 0:d], heads, tq)
        _fill_new_rows(kn, qkv_ref[:, d:2 * d], heads, tq)
        _fill_new_rows(vn, qkv_ref[:, 2 * d:3 * d], heads, tq)
        acc[...] = jnp.zeros_like(acc)
        run_ref[...] = jnp.zeros_like(run_ref)
        s_idx = lax.broadcasted_iota(jnp.int32, (KEY_BLOCK, rows), 0)
        t_idx = lax.broadcasted_iota(jnp.int32, (KEY_BLOCK, rows), 1) // heads
        process(kn[...], vn[...], s_idx < t_idx)

    @pl.when(p > 0)
    def _():
        process(kc_ref[...], vc_ref[...], None)

    @pl.when(p == n_pages)
    def _():
        _store_by_head(o_ref, acc[...], heads, tq)


def sb_sample_attention(qkv3, cache_k, cache_v, layer, page_table, bias):
    b, tq, d3 = qkv3.shape
    d = d3 // 3
    heads = d // HEAD_DIM
    rows = tq * heads
    n_pages = page_table.shape[1]
    page = cache_k.shape[2]
    assert page == KEY_BLOCK and tq <= KEY_BLOCK
    bias_rows = jnp.tile(bias.astype(F32), tq).reshape(1, rows)

    def page_map(bb, p, pt):
        return (layer, pt[bb, n_pages - jnp.maximum(p, 1)], 0, 0, 0)

    grid_spec = pltpu.PrefetchScalarGridSpec(
        num_scalar_prefetch=1,
        grid=(b, n_pages + 1),
        in_specs=[pl.BlockSpec((1, rows), lambda bb, p, pt: (0, 0)),
                  pl.BlockSpec((None, tq, d3), lambda bb, p, pt: (bb, 0, 0)),
                  pl.BlockSpec((None, None, page, heads, HEAD_DIM), page_map),
                  pl.BlockSpec((None, None, page, heads, HEAD_DIM), page_map)],
        out_specs=pl.BlockSpec((None, tq, d), lambda bb, p, pt: (bb, 0, 0)),
        scratch_shapes=[pltpu.VMEM((rows, HEAD_DIM), F32),
                        pltpu.VMEM((KEY_BLOCK, heads, HEAD_DIM), F32),
                        pltpu.VMEM((KEY_BLOCK, heads, HEAD_DIM), F32),
                        pltpu.VMEM((rows, HEAD_DIM), F32),
                        pltpu.VMEM((1, rows), F32),
                        pltpu.VMEM((KEY_BLOCK, rows), F32)])
    return pl.pallas_call(
        functools.partial(_sb_sample_kernel, heads=heads, tq=tq, n_pages=n_pages),
        out_shape=jax.ShapeDtypeStruct((b, tq, d), F32),
        grid_spec=grid_spec,
        compiler_params=_params(("parallel", "arbitrary"), 40 << 20),
        name="sb_sample",
    )(page_table, bias_rows, qkv3, cache_k, cache_v)


def _s5_discretize(lam_re, lam_im, log_dt):
    dt = jnp.exp(log_dt)
    mag = jnp.exp(lam_re * dt)
    ab_re = mag * jnp.cos(lam_im * dt)
    ab_im = mag * jnp.sin(lam_im * dt)
    den = lam_re * lam_re + lam_im * lam_im
    nr = ab_re - 1.0
    ni = ab_im
    f_re = (nr * lam_re + ni * lam_im) / den
    f_im = (ni * lam_re - nr * lam_im) / den
    return ab_re, ab_im, f_re, f_im


def _s5_tile_mask():
    r = lax.broadcasted_iota(jnp.int32, (HEAD_DIM, S5_TILE_STATE), 0) // S5_GROUP
    c = lax.broadcasted_iota(jnp.int32, (HEAD_DIM, S5_TILE_STATE), 1) // S5_STATE
    return r == c


def _s5_bbar(bt_re, bt_im, f_re, f_im):
    m = _s5_tile_mask()
    bb_re = jnp.where(m, f_re * bt_re - f_im * bt_im, 0.0)
    bb_im = jnp.where(m, f_re * bt_im + f_im * bt_re, 0.0)
    return bb_re, bb_im


def _gelu_tanh(x):
    return 0.5 * x * (1.0 + jnp.tanh(0.7978845608028654 * (x + 0.044715 * (x * x * x))))


def _s5_prompt_kernel(u_ref, lre_ref, lim_ref, ldt_ref, btr_ref, bti_ref, cr_ref, ci_ref, d_ref,
                      y_ref, hre_ref, him_ref,
                      toep, bend, cpow, ucat, sloc, hprev, yacc, ys, *, batch, seq):
    nch = seq // S5_CHUNK
    ns = S5_TILE_STATE
    ab_re, ab_im, f_re, f_im = _s5_discretize(lre_ref[...], lim_ref[...], ldt_ref[...])
    bb_re, bb_im = _s5_bbar(btr_ref[...], bti_ref[...], f_re, f_im)
    m = _s5_tile_mask()
    c_re = jnp.where(m, cr_ref[...], 0.0)
    c_im = jnp.where(m, ci_ref[...], 0.0)
    c_cat = jnp.concatenate([c_re, -c_im], axis=1).astype(BF16)

    toep[...] = jnp.zeros_like(toep)
    p_re = jnp.ones_like(ab_re)
    p_im = jnp.zeros_like(ab_im)
    for tau in range(S5_CHUNK):
        bt_cat = jnp.concatenate([bb_re * p_re - bb_im * p_im, bb_re * p_im + bb_im * p_re],
                                 axis=1).astype(BF16)
        s_blk = S5_CHUNK - 1 - tau
        bend[s_blk * HEAD_DIM:(s_blk + 1) * HEAD_DIM, :] = bt_cat
        k_tau = lax.dot_general(bt_cat, c_cat, _NT, preferred_element_type=F32).astype(BF16)
        for s in range(S5_CHUNK - tau):
            t = s + tau
            toep[s * HEAD_DIM:(s + 1) * HEAD_DIM, t * HEAD_DIM:(t + 1) * HEAD_DIM] = k_tau
        p_re, p_im = p_re * ab_re - p_im * ab_im, p_re * ab_im + p_im * ab_re
        cpow[tau * HEAD_DIM:(tau + 1) * HEAD_DIM, :] = jnp.concatenate(
            [c_re * p_re - c_im * p_im, -(c_re * p_im + c_im * p_re)], axis=1).astype(BF16)
    a16_re, a16_im = p_re, p_im

    for b in range(batch):
        for s in range(S5_CHUNK):
            ucat[b * nch:(b + 1) * nch, s * HEAD_DIM:(s + 1) * HEAD_DIM] = (
                u_ref[b, pl.ds(s, nch, stride=S5_CHUNK), :].astype(BF16))
    uc = ucat[...]
    yacc[...] = jnp.dot(uc, toep[...], preferred_element_type=F32)
    sloc[...] = jnp.dot(uc, bend[...], preferred_element_type=F32)

    def step(n, carry):
        out = []
        for b in range(batch):
            h_re, h_im = carry[2 * b], carry[2 * b + 1]
            r = b * nch + n
            hprev[pl.ds(r, 1), 0:ns] = h_re
            hprev[pl.ds(r, 1), ns:2 * ns] = h_im
            s_re = sloc[pl.ds(r, 1), 0:ns]
            s_im = sloc[pl.ds(r, 1), ns:2 * ns]
            out.append(a16_re * h_re - a16_im * h_im + s_re)
            out.append(a16_re * h_im + a16_im * h_re + s_im)
        return tuple(out)

    zero = jnp.zeros((1, ns), F32)
    fin = lax.fori_loop(0, nch, step, (zero,) * (2 * batch))
    for b in range(batch):
        hre_ref[b:b + 1, :] = fin[2 * b]
        him_ref[b:b + 1, :] = fin[2 * b + 1]

    yacc[...] += lax.dot_general(hprev[...].astype(BF16), cpow[...], _NT,
                                 preferred_element_type=F32)
    dsk = d_ref[...]
    for b in range(batch):
        for t in range(S5_CHUNK):
            rows = pl.ds(t, nch, stride=S5_CHUNK)
            y = (yacc[b * nch:(b + 1) * nch, t * HEAD_DIM:(t + 1) * HEAD_DIM]
                 + dsk * u_ref[b, rows, :])
            ys[b, rows, :] = _gelu_tanh(y)
    y_ref[...] = ys[...].astype(y_ref.dtype)


def _s5_tile_params(a_re, a_im, log_dt, b_re, b_im, c_re, c_im, d_skip):
    g, p = a_re.shape
    nt = g // S5_TILE_GROUPS
    lre = a_re.astype(F32).reshape(nt, 1, S5_TILE_STATE)
    lim = a_im.astype(F32).reshape(nt, 1, S5_TILE_STATE)
    ldt = jnp.repeat(log_dt.astype(F32), p).reshape(nt, 1, S5_TILE_STATE)

    def rows_gc(x):
        x = x.astype(F32).reshape(nt, HEAD_DIM, p)
        return jnp.tile(x, (1, 1, S5_TILE_GROUPS))

    btr = rows_gc(jnp.swapaxes(b_re, 1, 2))
    bti = rows_gc(jnp.swapaxes(b_im, 1, 2))
    cr = rows_gc(c_re)
    ci = rows_gc(c_im)
    dd = d_skip.astype(F32).reshape(1, g * S5_GROUP)
    return lre, lim, ldt, btr, bti, cr, ci, dd


def _s5_param_specs(n_lead):
    lead = (lambda i: (i, 0, 0))
    row = pl.BlockSpec((None, 1, S5_TILE_STATE), lead)
    mat = pl.BlockSpec((None, HEAD_DIM, S5_TILE_STATE), lead)
    del n_lead
    return [row, row, row, mat, mat, mat, mat, pl.BlockSpec((1, HEAD_DIM), lambda i: (0, i))]


def s5_prompt(u, tile_params):
    b, t, d = u.shape
    nt = d // HEAD_DIM
    nch = t // S5_CHUNK
    assert t % S5_CHUNK == 0 and nch % 8 == 0
    wide = S5_CHUNK * HEAD_DIM
    ns2 = 2 * S5_TILE_STATE
    return pl.pallas_call(
        functools.partial(_s5_prompt_kernel, batch=b, seq=t),
        out_shape=(jax.ShapeDtypeStruct((b, t, d), BF16),
                   jax.ShapeDtypeStruct((b, nt * S5_TILE_STATE), F32),
                   jax.ShapeDtypeStruct((b, nt * S5_TILE_STATE), F32)),
        grid=(nt,),
        in_specs=[pl.BlockSpec((b, t, HEAD_DIM), lambda i: (0, 0, i))] + _s5_param_specs(1),
        out_specs=(pl.BlockSpec((b, t, HEAD_DIM), lambda i: (0, 0, i)),
                   pl.BlockSpec((b, S5_TILE_STATE), lambda i: (0, i)),
                   pl.BlockSpec((b, S5_TILE_STATE), lambda i: (0, i))),
        scratch_shapes=[pltpu.VMEM((wide, wide), BF16),
                        pltpu.VMEM((wide, ns2), BF16),
                        pltpu.VMEM((wide, ns2), BF16),
                        pltpu.VMEM((b * nch, wide), BF16),
                        pltpu.VMEM((b * nch, ns2), F32),
                        pltpu.VMEM((b * nch, ns2), F32),
                        pltpu.VMEM((b * nch, wide), F32),
                        pltpu.VMEM((b, t, HEAD_DIM), F32)],
        compiler_params=_params(("parallel",), 58 << 20),
        name="s5_prompt",
    )(u, *tile_params)


def _s5_sample_kernel(u_ref, h0r_ref, h0i_ref, lre_ref, lim_ref, ldt_ref, btr_ref, bti_ref,
                      cr_ref, ci_ref, d_ref, y_ref, hre_ref, him_ref, *, tq):
    ab_re, ab_im, f_re, f_im = _s5_discretize(lre_ref[...], lim_ref[...], ldt_ref[...])
    bb_re, bb_im = _s5_bbar(btr_ref[...], bti_ref[...], f_re, f_im)
    m = _s5_tile_mask()
    c_cat = jnp.concatenate([jnp.where(m, cr_ref[...], 0.0), -jnp.where(m, ci_ref[...], 0.0)],
                            axis=1).astype(BF16)
    b_cat = jnp.concatenate([bb_re, bb_im], axis=1).astype(BF16)
    ns = S5_TILE_STATE
    h_re = h0r_ref[...]
    h_im = h0i_ref[...]
    dsk = d_ref[...]
    for t in range(tq):
        u = u_ref[t]
        bu = jnp.dot(u.astype(BF16), b_cat, preferred_element_type=F32)
        h_re, h_im = (ab_re * h_re - ab_im * h_im + bu[:, 0:ns],
                      ab_re * h_im + ab_im * h_re + bu[:, ns:2 * ns])
        h_cat = jnp.concatenate([h_re, h_im], axis=1).astype(BF16)
        y = lax.dot_general(h_cat, c_cat, _NT, preferred_element_type=F32) + dsk * u
        y_ref[t] = _gelu_tanh(y)
    hre_ref[...] = h_re
    him_ref[...] = h_im


def s5_sample(u_tb, h0_re, h0_im, tile_params):
    tq, b, d = u_tb.shape
    nt = d // HEAD_DIM
    st = pl.BlockSpec((b, S5_TILE_STATE), lambda i: (0, i))
    ublk = pl.BlockSpec((tq, b, HEAD_DIM), lambda i: (0, 0, i))
    return pl.pallas_call(
        functools.partial(_s5_sample_kernel, tq=tq),
        out_shape=(jax.ShapeDtypeStruct((tq, b, d), F32),
                   jax.ShapeDtypeStruct((b, nt * S5_TILE_STATE), F32),
                   jax.ShapeDtypeStruct((b, nt * S5_TILE_STATE), F32)),
        grid=(nt,),
        in_specs=[ublk, st, st] + _s5_param_specs(1),
        out_specs=(ublk, st, st),
        compiler_params=_params(("parallel",)),
        name="s5_sample",
    )(u_tb, h0_re, h0_im, *tile_params)


def _band_prompt_kernel(*refs, seq, groups):
    ng = len(groups)
    qkv = refs[:3 * ng]
    o_ref = refs[3 * ng]
    og, lg = refs[3 * ng + 1], refs[3 * ng + 2]
    kb = KEY_BLOCK
    a1 = lax.broadcasted_iota(jnp.int32, (kb, kb), 0)
    c1 = lax.broadcasted_iota(jnp.int32, (kb, kb), 1)
    a2 = lax.broadcasted_iota(jnp.int32, (kb, 2 * kb), 0)
    c2 = lax.broadcasted_iota(jnp.int32, (kb, 2 * kb), 1)
    first_valid = c1 <= a1
    band_valid = (c2 >= a2) & (c2 <= a2 + kb)

    for g, (win, dil) in enumerate(groups):
        assert win // dil == kb and seq % (dil * kb) == 0
        q_ref, k_ref, v_ref = qkv[3 * g], qkv[3 * g + 1], qkv[3 * g + 2]
        nb = seq // (dil * kb)

        def rows(r, n):
            if dil == 1:
                return pl.ds(n * kb, kb)
            return pl.ds(r + n * kb * dil, kb, stride=dil)

        for r in range(dil):
            for n in range(nb):
                qb = q_ref[rows(r, n), :].astype(BF16)
                if n == 0:
                    kk = k_ref[rows(r, 0), :].astype(BF16)
                    vv = v_ref[rows(r, 0), :].astype(BF16)
                    valid = first_valid
                else:
                    kk = jnp.concatenate([k_ref[rows(r, n - 1), :], k_ref[rows(r, n), :]],
                                         axis=0).astype(BF16)
                    vv = jnp.concatenate([v_ref[rows(r, n - 1), :], v_ref[rows(r, n), :]],
                                         axis=0).astype(BF16)
                    valid = band_valid
                s = lax.dot_general(qb, kk, _NT, preferred_element_type=F32) * ATTN_SCALE
                s = jnp.where(valid, s, -jnp.inf)
                mx = jnp.max(s, axis=-1, keepdims=True)
                p = jnp.exp(s - mx)
                l = jnp.sum(p, axis=-1, keepdims=True)
                o = jnp.dot(p.astype(BF16), vv, preferred_element_type=F32) / l
                og[g, rows(r, n), :] = o
                lg[g, rows(r, n), :] = jnp.broadcast_to(mx + jnp.log(l), (kb, HEAD_DIM))

    lses = [lg[g] for g in range(ng)]
    top = functools.reduce(jnp.maximum, lses)
    es = [jnp.exp(x - top) for x in lses]
    num = functools.reduce(lambda a, b: a + b, [e * og[g] for g, e in enumerate(es)])
    den = functools.reduce(lambda a, b: a + b, es)
    o_ref[...] = (num / den).astype(o_ref.dtype)


def band_prompt_attention(qkv, batch, seq, heads, groups):
    ng = len(groups)
    d = heads * HEAD_DIM

    def spec(c):
        return pl.BlockSpec((seq, HEAD_DIM), lambda b, h: (b, c * heads + h))

    return pl.pallas_call(
        functools.partial(_band_prompt_kernel, seq=seq, groups=groups),
        out_shape=jax.ShapeDtypeStruct((batch * seq, d), BF16),
        grid=(batch, heads),
        in_specs=[spec(c) for c in range(3 * ng)],
        out_specs=pl.BlockSpec((seq, HEAD_DIM), lambda b, h: (b, h)),
        scratch_shapes=[pltpu.VMEM((ng, seq, HEAD_DIM), F32),
                        pltpu.VMEM((ng, seq, HEAD_DIM), F32)],
        compiler_params=_params(("parallel", "parallel"), 48 << 20),
        name="band_prompt",
    )(*([qkv] * (3 * ng)))


def _band_sample_kernel(qkv_ref, kb_ref, vb_ref, o_ref, lse_ref, qd, kn, vn, acc, mx_ref, l_ref,
                        z_ref, *, heads, tq, group, dil, n_buf):
    c = pl.program_id(1)
    d = heads * HEAD_DIM
    rows = tq * heads
    key = lax.broadcasted_iota(jnp.int32, (KEY_BLOCK, rows), 0)
    tok = lax.broadcasted_iota(jnp.int32, (KEY_BLOCK, rows), 1) // heads
    if dil == 1:
        new_valid = key <= tok
        buf_valid = key >= tok
    else:
        new_valid = key == tok
        buf_valid = tok == c - 1

    def process(k3, v3, valid):
        s = _key_logits(k3, qd[...].astype(BF16), heads, z_ref) * ATTN_SCALE
        s = jnp.where(valid, s, -jnp.inf)
        m_old = mx_ref[...]
        m_new = jnp.maximum(m_old, jnp.max(s, axis=0, keepdims=True))
        p = jnp.exp(s - m_new)
        alpha = jnp.exp(m_old - m_new)
        l_ref[...] = alpha * l_ref[...] + jnp.sum(p, axis=0, keepdims=True)
        mx_ref[...] = m_new
        acc[...] = _lanes_to_rows(alpha) * acc[...] + _weighted_values(p, v3, heads)

    @pl.when(c == 0)
    def _():
        base = group * 3 * d
        _fill_queries(qd, qkv_ref[:, base:base + d], heads, tq)
        _fill_new_rows(kn, qkv_ref[:, base + d:base + 2 * d], heads, tq)
        _fill_new_rows(vn, qkv_ref[:, base + 2 * d:base + 3 * d], heads, tq)
        acc[...] = jnp.zeros_like(acc)
        mx_ref[...] = jnp.full_like(mx_ref, NEG_BIG)
        l_ref[...] = jnp.zeros_like(l_ref)
        process(kn[...], vn[...], new_valid)

    @pl.when(c > 0)
    def _():
        process(kb_ref[...], vb_ref[...], buf_valid)

    @pl.when(c == n_buf)
    def _():
        l = l_ref[...]
        _store_by_head(o_ref, acc[...] / _lanes_to_rows(l), heads, tq)
        _store_by_head(lse_ref, _lanes_to_rows(mx_ref[...] + jnp.log(l)), heads, tq)


def band_sample_attention(qkv3, buf_k, buf_v, layer, group, win, dil):
    b, tq, dall = qkv3.shape
    nc, _, wb, heads, hd = buf_k.shape
    d = heads * hd
    assert wb == win and win // dil == KEY_BLOCK and tq <= KEY_BLOCK
    if dil == 1:
        kv = buf_k, buf_v
        n_buf = 1
        last = wb // KEY_BLOCK - 1
        blk = pl.BlockSpec((None, None, KEY_BLOCK, heads, hd),
                           lambda bb, c: (layer, bb, last, 0, 0))
    else:
        assert tq <= dil
        shape = (nc, b, wb // dil, dil, heads, hd)
        kv = buf_k.reshape(shape), buf_v.reshape(shape)
        n_buf = tq
        blk = pl.BlockSpec((None, None, KEY_BLOCK, None, heads, hd),
                           lambda bb, c: (layer, bb, 0, jnp.maximum(c - 1, 0), 0, 0))
    rows = tq * heads
    out = pl.BlockSpec((None, tq, d), lambda bb, c: (bb, 0, 0))
    return pl.pallas_call(
        functools.partial(_band_sample_kernel, heads=heads, tq=tq, group=group, dil=dil,
                          n_buf=n_buf),
        out_shape=(jax.ShapeDtypeStruct((b, tq, d), F32), jax.ShapeDtypeStruct((b, tq, d), F32)),
        grid=(b, n_buf + 1),
        in_specs=[pl.BlockSpec((None, tq, dall), lambda bb, c: (bb, 0, 0)), blk, blk],
        out_specs=(out, out),
        scratch_shapes=[pltpu.VMEM((rows, HEAD_DIM), F32),
                        pltpu.VMEM((KEY_BLOCK, heads, HEAD_DIM), F32),
                        pltpu.VMEM((KEY_BLOCK, heads, HEAD_DIM), F32),
                        pltpu.VMEM((rows, HEAD_DIM), F32),
                        pltpu.VMEM((1, rows), F32),
                        pltpu.VMEM((1, rows), F32),
                        pltpu.VMEM((KEY_BLOCK, rows), F32)],
        compiler_params=_params(("parallel", "arbitrary"), 40 << 20),
        name="band_sample",
    )(qkv3, *kv)


def _merge_groups_kernel(*refs, ng):
    outs, lses, o_ref = refs[:ng], refs[ng:2 * ng], refs[2 * ng]
    ls = [r[...] for r in lses]
    top = functools.reduce(jnp.maximum, ls)
    es = [jnp.exp(x - top) for x in ls]
    num = functools.reduce(lambda a, b: a + b, [e * r[...] for e, r in zip(es, outs)])
    den = functools.reduce(lambda a, b: a + b, es)
    o_ref[...] = (num / den).astype(o_ref.dtype)


def merge_groups(outs, lses):
    m, d = outs[0].shape
    return pl.pallas_call(
        functools.partial(_merge_groups_kernel, ng=len(outs)),
        out_shape=jax.ShapeDtypeStruct((m, d), BF16),
        name="merge_groups",
    )(*outs, *lses)


def _ffn(xp, xs, layer, norm_ffn, ffn_w1, ffn_w2):
    hp = rmsnorm(xp, norm_ffn, layer)
    hs = rmsnorm(xs, norm_ffn, layer)
    hid_p, hid_s = matmul(hp, hs, [ffn_w1], layer, ffn_w1.shape[-1], epi="relu2", out_dtype=BF16,
                          **_WIDE_TILE)
    return matmul_ktiled(hid_p, hid_s, ffn_w2, layer, xp, xs)


def _qkv_gain(q_gain, k_gain, d):
    reps = d // HEAD_DIM
    return jnp.concatenate([jnp.tile(q_gain.astype(F32), reps), jnp.tile(k_gain.astype(F32), reps),
                            jnp.ones((d,), F32)])


def kernel(x_prompt, x_sample, cache_a_k, cache_a_v, state_b_re, state_b_im, cache_c_k_w128, cache_c_v_w128, cache_c_k_w512, cache_c_v_w512, cache_c_k_w2048, cache_c_v_w2048, page_table, norm_mix, norm_ffn, a_w_qkv, a_q_gain, a_k_gain, a_logit_bias, a_w_o, b_w_in, b_a_re, b_a_im, b_log_dt, b_b_re, b_b_im, b_c_re, b_c_im, b_d, b_w_glu, b_w_gate, c_w_qkv, c_q_gain, c_k_gain, c_w_o, ffn_w1, ffn_w2):
    bp, tp, d = x_prompt.shape
    bs, ts, _ = x_sample.shape
    heads = d // HEAD_DIM
    depth = norm_mix.shape[0]
    c_buf_k = (cache_c_k_w128, cache_c_k_w512, cache_c_k_w2048)
    c_buf_v = (cache_c_v_w128, cache_c_v_w512, cache_c_v_w2048)
    ng = len(C_GROUPS)
    xp = x_prompt.reshape(bp * tp, d)
    xs = x_sample.reshape(bs * ts, d)
    ak_p, av_p, ak_s, av_s = [], [], [], []
    br_p, bi_p, br_s, bi_s = [], [], [], []
    ck_p = [[] for _ in C_GROUPS]
    cv_p = [[] for _ in C_GROUPS]
    ck_s = [[] for _ in C_GROUPS]
    cv_s = [[] for _ in C_GROUPS]
    ia = ib = ic = 0
    for layer in range(depth):
        hp = rmsnorm(xp, norm_mix, layer)
        hs = rmsnorm(xs, norm_mix, layer)
        kind = layer % N_MIXERS
        if kind == 0:
            gain = _qkv_gain(a_q_gain[ia], a_k_gain[ia], d).reshape(1, 3 * d)
            qkv_p, qkv_s = matmul(hp, hs, [a_w_qkv], ia, 3 * d, epi="headnorm", gain=gain,
                                  **_WIDE_TILE)
            op = sb_prompt_attention(qkv_p, a_logit_bias[ia].astype(F32), bp, tp, heads)
            os_ = sb_sample_attention(qkv_s.reshape(bs, ts, 3 * d), cache_a_k, cache_a_v, ia,
                                      page_table, a_logit_bias[ia])
            xp, xs = matmul(op, os_.reshape(bs * ts, d).astype(BF16), [a_w_o], ia, d,
                            res=xp, res_s=xs)
            ak_p.append(qkv_p[:, d:2 * d].reshape(bp, tp, heads, HEAD_DIM))
            av_p.append(qkv_p[:, 2 * d:].reshape(bp, tp, heads, HEAD_DIM))
            ak_s.append(qkv_s[:, d:2 * d].reshape(bs, ts, heads, HEAD_DIM))
            av_s.append(qkv_s[:, 2 * d:].reshape(bs, ts, heads, HEAD_DIM))
            ia += 1
        elif kind == 1:
            tile_params = _s5_tile_params(b_a_re[ib], b_a_im[ib], b_log_dt[ib], b_b_re[ib],
                                          b_b_im[ib], b_c_re[ib], b_c_im[ib], b_d[ib])
            up, us = matmul(hp, hs, [b_w_in], ib, d, **_WIDE_TILE)
            yp, rp, ip = s5_prompt(up.reshape(bp, tp, d), tile_params)
            us_tb = jnp.swapaxes(us.reshape(bs, ts, d), 0, 1)
            ys_tb, rn, im_ = s5_sample(us_tb, state_b_re[ib].reshape(bs, -1),
                                       state_b_im[ib].reshape(bs, -1), tile_params)
            ys = jnp.swapaxes(ys_tb, 0, 1).reshape(bs * ts, d).astype(BF16)
            xp, xs = matmul(yp.reshape(bp * tp, d), ys, [b_w_glu, b_w_gate], ib, d, epi="glu",
                            res=xp, res_s=xs, bm=512, bn=512)
            g = d // S5_GROUP
            br_p.append(rp.reshape(bp, g, S5_STATE))
            bi_p.append(ip.reshape(bp, g, S5_STATE))
            br_s.append(rn.reshape(bs, g, S5_STATE))
            bi_s.append(im_.reshape(bs, g, S5_STATE))
            ib += 1
        else:
            gain = jnp.concatenate([_qkv_gain(c_q_gain[ic, gi], c_k_gain[ic, gi], d)
                                    for gi in range(ng)]).reshape(1, ng * 3 * d)
            qkv_p, qkv_s = matmul(hp, hs, [c_w_qkv], ic, ng * 3 * d, epi="headnorm", gain=gain,
                                  **_WIDE_TILE)
            op = band_prompt_attention(qkv_p, bp, tp, heads, C_GROUPS)
            qkv_s3 = qkv_s.reshape(bs, ts, ng * 3 * d)
            outs, lses = [], []
            for gi, (win, dil) in enumerate(C_GROUPS):
                o_g, l_g = band_sample_attention(qkv_s3, c_buf_k[gi], c_buf_v[gi], ic, gi, win, dil)
                outs.append(o_g.reshape(bs * ts, d))
                lses.append(l_g.reshape(bs * ts, d))
            os_ = merge_groups(outs, lses)
            xp, xs = matmul(op, os_, [c_w_o], ic, d, res=xp, res_s=xs)
            qp5 = qkv_p.reshape(bp, tp, ng, 3, heads, HEAD_DIM)
            qs5 = qkv_s.reshape(bs, ts, ng, 3, heads, HEAD_DIM)
            for gi, (win, dil) in enumerate(C_GROUPS):
                keep = min(win, tp)
                ck_p[gi].append(qp5[:, tp - keep:, gi, 1])
                cv_p[gi].append(qp5[:, tp - keep:, gi, 2])
                wb = c_buf_k[gi].shape[2]
                ck_s[gi].append(jnp.concatenate([c_buf_k[gi][ic], qs5[:, :, gi, 1]], axis=1)[:, -wb:])
                cv_s[gi].append(jnp.concatenate([c_buf_v[gi][ic], qs5[:, :, gi, 2]], axis=1)[:, -wb:])
            ic += 1
        xp, xs = _ffn(xp, xs, layer, norm_ffn, ffn_w1, ffn_w2)
    return (xp.reshape(bp, tp, d), xs.reshape(bs, ts, d),
            jnp.stack(ak_p), jnp.stack(av_p), jnp.stack(ak_s), jnp.stack(av_s),
            jnp.stack(br_p), jnp.stack(bi_p), jnp.stack(br_s), jnp.stack(bi_s),
            jnp.stack(ck_p[0]), jnp.stack(cv_p[0]), jnp.stack(ck_p[1]), jnp.stack(cv_p[1]),
            jnp.stack(ck_p[2]), jnp.stack(cv_p[2]),
            jnp.stack(ck_s[0]), jnp.stack(cv_s[0]), jnp.stack(ck_s[1]), jnp.stack(cv_s[1]),
            jnp.stack(ck_s[2]), jnp.stack(cv_s[2]))
```

```python
import functools
import math

import jax
import jax.numpy as jnp
from jax import lax
from jax.experimental import pallas as pl
from jax.experimental.pallas import tpu as pltpu

F32 = jnp.float32
BF16 = jnp.bfloat16

HEAD_DIM = 128
KEY_BLOCK = 128
EPS = 1e-6
ATTN_SCALE = HEAD_DIM ** -0.5
N_MIXERS = 3
S5_GROUP = 16
S5_STATE = 64
S5_TILE_GROUPS = HEAD_DIM // S5_GROUP
S5_TILE_STATE = S5_TILE_GROUPS * S5_STATE
S5_CHUNK = 16
C_GROUPS = ((128, 1), (512, 4), (2048, 16))
VMEM_CAP = 60 * 1024 * 1024
NEG_BIG = -1e30

_NT = (((1,), (1,)), ((), ()))


def _params(sem, vmem_bytes=None):
    limit = None if vmem_bytes is None else int(min(VMEM_CAP, vmem_bytes))
    return pltpu.CompilerParams(dimension_semantics=sem, vmem_limit_bytes=limit)


def _rmsnorm_kernel(x_ref, g_ref, o_ref):
    x = x_ref[...]
    ms = jnp.mean(x * x, axis=-1, keepdims=True)
    o_ref[...] = (x * lax.rsqrt(ms + EPS) * g_ref[...]).astype(o_ref.dtype)


def rmsnorm(x, gains, layer):
    m, d = x.shape
    bm = min(m, 256)
    g3 = gains.reshape(gains.shape[0], 1, d)
    return pl.pallas_call(
        _rmsnorm_kernel,
        out_shape=jax.ShapeDtypeStruct((m, d), BF16),
        grid=(m // bm,),
        in_specs=[pl.BlockSpec((bm, d), lambda i: (i, 0)),
                  pl.BlockSpec((None, 1, d), lambda i: (layer, 0, 0))],
        out_specs=pl.BlockSpec((bm, d), lambda i: (i, 0)),
        compiler_params=_params(("parallel",)),
        name="rmsnorm",
    )(x, g3)


def _mm_tile(x, wbf, epi, sec, gain_ref, res_ref, o_ref):
    acc = jnp.dot(x, wbf[0][...], preferred_element_type=F32)
    if epi == "glu":
        acc = acc * jax.nn.sigmoid(jnp.dot(x, wbf[1][...], preferred_element_type=F32))
    elif epi == "relu2":
        r = jnp.maximum(acc, 0.0)
        acc = r * r
    if epi == "headnorm":
        @pl.when(sec < 2)
        def _():
            for c in range(acc.shape[1] // HEAD_DIM):
                sl = slice(c * HEAD_DIM, (c + 1) * HEAD_DIM)
                y = acc[:, sl]
                ms = jnp.mean(y * y, axis=-1, keepdims=True)
                o_ref[:, sl] = (y * lax.rsqrt(ms + EPS) * gain_ref[:, sl]).astype(o_ref.dtype)

        @pl.when(sec == 2)
        def _():
            o_ref[...] = acc.astype(o_ref.dtype)
    else:
        if res_ref is not None:
            acc = res_ref[...] + acc
        o_ref[...] = acc.astype(o_ref.dtype)


def _mm_kernel(*refs, n_w, epi, has_res, sec_blocks):
    it = iter(refs)
    x_ref, xs_ref = next(it), next(it)
    w_refs = [next(it) for _ in range(n_w)]
    gain_ref = next(it) if epi == "headnorm" else None
    res_ref, ress_ref = (next(it), next(it)) if has_res else (None, None)
    o_ref, os_ref = next(it), next(it)
    wbf = [next(it) for _ in range(n_w)]
    sec = (pl.program_id(0) // sec_blocks) % 3

    @pl.when(pl.program_id(1) == 0)
    def _():
        for w_ref, s in zip(w_refs, wbf):
            s[...] = w_ref[...].astype(BF16)
        _mm_tile(xs_ref[...], wbf, epi, sec, gain_ref, ress_ref, os_ref)

    _mm_tile(x_ref[...], wbf, epi, sec, gain_ref, res_ref, o_ref)


def matmul(x, xs, ws, layer, n_out, *, epi="none", res=None, res_s=None, gain=None,
           out_dtype=F32, bm=1024, bn=512):
    m, k = x.shape
    ms = xs.shape[0]
    n_w = len(ws)
    bm = min(m, bm)
    bn = math.gcd(bn, n_out, k if epi == "headnorm" else n_out)
    assert m % bm == 0 and bn % HEAD_DIM == 0
    grid = (n_out // bn, m // bm)
    in_specs = [pl.BlockSpec((bm, k), lambda j, i: (i, 0)),
                pl.BlockSpec((ms, k), lambda j, i: (0, 0))]
    in_specs += [pl.BlockSpec((None, k, bn), lambda j, i: (layer, 0, j)) for _ in ws]
    args = [x, xs, *ws]
    sec_blocks = 1
    if epi == "headnorm":
        assert k % bn == 0
        sec_blocks = k // bn
        in_specs.append(pl.BlockSpec((1, bn), lambda j, i: (0, j)))
        args.append(gain)
    if res is not None:
        in_specs += [pl.BlockSpec((bm, bn), lambda j, i: (i, j)),
                     pl.BlockSpec((ms, bn), lambda j, i: (0, j))]
        args += [res, res_s]
    out_bytes = jnp.dtype(out_dtype).itemsize
    vmem = (2 * (bm + ms) * k * 2 + n_w * (2 * k * bn * 4 + k * bn * 2)
            + 2 * (bm + ms) * bn * (out_bytes + (4 if res is not None else 0))
            + (2 + n_w) * bm * bn * 4 + (4 << 20))
    return pl.pallas_call(
        functools.partial(_mm_kernel, n_w=n_w, epi=epi, has_res=res is not None,
                          sec_blocks=sec_blocks),
        out_shape=(jax.ShapeDtypeStruct((m, n_out), out_dtype),
                   jax.ShapeDtypeStruct((ms, n_out), out_dtype)),
        grid=grid,
        in_specs=in_specs,
        out_specs=(pl.BlockSpec((bm, bn), lambda j, i: (i, j)),
                   pl.BlockSpec((ms, bn), lambda j, i: (0, j))),
        scratch_shapes=[pltpu.VMEM((k, bn), BF16) for _ in ws],
        compiler_params=_params(("parallel", "arbitrary"), vmem),
        name="mm_" + epi,
    )(*args)


def _mm_kt_kernel(x_ref, xs_ref, w_ref, res_ref, ress_ref, o_ref, os_ref):
    i = pl.program_id(1)
    kk = pl.program_id(2)
    w = w_ref[...].astype(BF16)

    @pl.when(kk == 0)
    def _():
        o_ref[...] = res_ref[...]

    o_ref[...] += jnp.dot(x_ref[...], w, preferred_element_type=F32)

    @pl.when(i == 0)
    def _():
        @pl.when(kk == 0)
        def _():
            os_ref[...] = ress_ref[...]

        os_ref[...] += jnp.dot(xs_ref[...], w, preferred_element_type=F32)


_WIDE_TILE = dict(bm=512, bn=1024)


def matmul_ktiled(x, xs, w, layer, res, res_s, *, bm=2048, bn=1024, bk=1024):
    m, k = x.shape
    ms = xs.shape[0]
    n = w.shape[-1]
    bm = min(m, bm)
    bn = min(bn, n)
    bk = min(bk, k)
    assert m % bm == 0 and n % bn == 0 and k % bk == 0
    vmem = (2 * (bm + ms) * bk * 2 + 2 * bk * bn * 4 + bk * bn * 2 + 5 * (bm + ms) * bn * 4
            + (4 << 20))
    return pl.pallas_call(
        _mm_kt_kernel,
        out_shape=(jax.ShapeDtypeStruct((m, n), F32), jax.ShapeDtypeStruct((ms, n), F32)),
        grid=(n // bn, m // bm, k // bk),
        in_specs=[pl.BlockSpec((bm, bk), lambda j, i, kk: (i, kk)),
                  pl.BlockSpec((ms, bk), lambda j, i, kk: (0, kk)),
                  pl.BlockSpec((None, bk, bn), lambda j, i, kk: (layer, kk, j)),
                  pl.BlockSpec((bm, bn), lambda j, i, kk: (i, j)),
                  pl.BlockSpec((ms, bn), lambda j, i, kk: (0, j))],
        out_specs=(pl.BlockSpec((bm, bn), lambda j, i, kk: (i, j)),
                   pl.BlockSpec((ms, bn), lambda j, i, kk: (0, j))),
        compiler_params=_params(("parallel", "arbitrary", "arbitrary"), vmem),
        name="mm_ktiled",
    )(x, xs, w, res, res_s)


def _log_keep(z):
    return -(jnp.maximum(z, 0.0) + jnp.log(1.0 + jnp.exp(-jnp.abs(z))))


def _sb_prompt_kernel(bias_ref, q_ref, k_ref, v_ref, o_ref, *, bq):
    h = pl.program_id(1)
    qi = pl.program_id(2)
    nsub = bq // KEY_BLOCK
    bias = bias_ref[h]
    q = (q_ref[...] * ATTN_SCALE).astype(BF16)
    key_j = lax.broadcasted_iota(jnp.int32, (KEY_BLOCK, 2 * KEY_BLOCK), 0)
    key_s = lax.broadcasted_iota(jnp.int32, (KEY_BLOCK, 2 * KEY_BLOCK), 1)
    cs = jnp.where((key_s >= KEY_BLOCK) | (key_j > key_s), 1.0, 0.0).astype(BF16)
    ahead = [lax.broadcasted_iota(jnp.int32, (bq, KEY_BLOCK), 1) + g * KEY_BLOCK
             - lax.broadcasted_iota(jnp.int32, (bq, KEY_BLOCK), 0) for g in range(nsub)]

    def group(j, run, acc, diagonal):
        ks = pl.multiple_of(j * bq, bq)
        kk = k_ref[pl.ds(ks, bq), :].astype(BF16)
        vv = v_ref[pl.ds(ks, bq), :].astype(BF16)
        z = lax.dot_general(q, kk, _NT, preferred_element_type=F32) + bias
        parts = []
        for g in range(nsub):
            zg = z[:, g * KEY_BLOCK:(g + 1) * KEY_BLOCK]
            lk = _log_keep(zg)
            if diagonal:
                lk = jnp.where(ahead[g] < 0, lk, 0.0)
            sums = jnp.dot(lk.astype(BF16), cs, preferred_element_type=F32)
            parts.append((zg + lk + sums[:, :KEY_BLOCK], sums[:, KEY_BLOCK:]))
        ws = [None] * nsub
        for g in reversed(range(nsub)):
            logw, total = parts[g]
            w = jnp.exp(logw + run)
            if diagonal:
                w = jnp.where(ahead[g] < 0, w, 0.0)
            ws[g] = w.astype(BF16)
            run = run + total
        acc = acc + jnp.dot(jnp.concatenate(ws, axis=1), vv, preferred_element_type=F32)
        return run, acc

    run = jnp.zeros((bq, KEY_BLOCK), F32)
    acc = jnp.zeros((bq, HEAD_DIM), F32)
    run, acc = group(qi, run, acc, True)

    def body(it, carry):
        return group(qi - 1 - it, carry[0], carry[1], False)

    run, acc = lax.fori_loop(0, qi, body, (run, acc))
    o_ref[...] = acc.astype(o_ref.dtype)


def sb_prompt_attention(qkv, bias, batch, seq, heads):
    bq = min(seq, 512)
    nq = seq // bq
    d = heads * HEAD_DIM
    return pl.pallas_call(
        functools.partial(_sb_prompt_kernel, bq=bq),
        out_shape=jax.ShapeDtypeStruct((batch * seq, d), BF16),
        grid=(batch, heads, nq),
        in_specs=[pl.BlockSpec(memory_space=pltpu.SMEM),
                  pl.BlockSpec((bq, HEAD_DIM), lambda b, h, i: (b * nq + i, h)),
                  pl.BlockSpec((seq, HEAD_DIM), lambda b, h, i: (b, heads + h)),
                  pl.BlockSpec((seq, HEAD_DIM), lambda b, h, i: (b, 2 * heads + h))],
        out_specs=pl.BlockSpec((bq, HEAD_DIM), lambda b, h, i: (b * nq + i, h)),
        compiler_params=_params(("parallel", "parallel", "arbitrary")),
        name="sb_prompt",
    )(bias, qkv, qkv, qkv)


def _head_match(heads, rows):
    return (lax.broadcasted_iota(jnp.int32, (heads, rows), 0)
            == lax.broadcasted_iota(jnp.int32, (heads, rows), 1) % heads)


def _fill_queries(qd, q, heads, tq):
    for t in range(tq):
        for h in range(heads):
            r = t * heads + h
            qd[r:r + 1, :] = q[t:t + 1, h * HEAD_DIM:(h + 1) * HEAD_DIM]


def _fill_new_rows(dst, src, heads, tq):
    dst[...] = jnp.zeros_like(dst)
    for t in range(tq):
        for h in range(heads):
            dst[t, h:h + 1, :] = src[t:t + 1, h * HEAD_DIM:(h + 1) * HEAD_DIM]


def _key_logits(k3, qd, heads, z_ref):
    rows = qd.shape[0]
    k2 = k3.reshape(KEY_BLOCK * heads, HEAD_DIM).astype(BF16)
    zt = lax.dot_general(k2, qd, _NT, preferred_element_type=F32).reshape(KEY_BLOCK, heads, rows)
    z_ref[...] = jnp.sum(jnp.where(_head_match(heads, rows)[None], zt, 0.0), axis=1)
    return z_ref[...]


def _weighted_values(w, v3, heads):
    rows = w.shape[1]
    wexp = jnp.where(_head_match(heads, rows)[None], w[:, None, :], 0.0)
    wexp = wexp.reshape(KEY_BLOCK * heads, rows).astype(BF16)
    v2 = v3.reshape(KEY_BLOCK * heads, HEAD_DIM).astype(BF16)
    return lax.dot_general(wexp, v2, (((0,), (0,)), ((), ())), preferred_element_type=F32)


def _lanes_to_rows(x):
    return jnp.transpose(jnp.broadcast_to(x, (HEAD_DIM, x.shape[1])))


def _store_by_head(o_ref, a, heads, tq):
    for t in range(tq):
        for h in range(heads):
            r = t * heads + h
            o_ref[t:t + 1, h * HEAD_DIM:(h + 1) * HEAD_DIM] = a[r:r + 1, :].astype(o_ref.dtype)


def _sb_sample_kernel(pt_ref, bias_ref, qkv_ref, k0_ref, k1_ref, v0_ref, v1_ref, o_ref,
                      qd, kn, vn, acc, run_ref, z_ref, *, heads, tq, n_steps):
    del pt_ref
    p = pl.program_id(1)
    d = heads * HEAD_DIM
    rows = tq * heads
    key = lax.broadcasted_iota(jnp.int32, (KEY_BLOCK, KEY_BLOCK), 0)
    other = lax.broadcasted_iota(jnp.int32, (KEY_BLOCK, KEY_BLOCK), 1)
    later = jnp.where(other > key, 1.0, 0.0).astype(BF16)

    def process(blocks):
        staged = []
        for n, (k3, v3, valid) in enumerate(blocks):
            z = (_key_logits(k3, qd[...].astype(BF16), heads, z_ref.at[n]) * ATTN_SCALE
                 + bias_ref[...])
            lk = _log_keep(z)
            if valid is not None:
                lk = jnp.where(valid, lk, 0.0)
            hi = lk.astype(BF16)
            lo = (lk - hi.astype(F32)).astype(BF16)
            sums = (jnp.dot(later, hi, preferred_element_type=F32)
                    + jnp.dot(later, lo, preferred_element_type=F32))
            staged.append((z + lk + sums, jnp.sum(lk, axis=0, keepdims=True), valid, v3))
        run = run_ref[...]
        out = acc[...]
        for logw, total, valid, v3 in staged:
            w = jnp.exp(logw + run)
            if valid is not None:
                w = jnp.where(valid, w, 0.0)
            run = run + total
            out = out + _weighted_values(w, v3, heads)
        run_ref[...] = run
        acc[...] = out

    @pl.when(p == 0)
    def _():
        _fill_queries(qd, qkv_ref[:, 0:d], heads, tq)
        _fill_new_rows(kn, qkv_ref[:, d:2 * d], heads, tq)
        _fill_new_rows(vn, qkv_ref[:, 2 * d:3 * d], heads, tq)
        acc[...] = jnp.zeros_like(acc)
        run_ref[...] = jnp.zeros_like(run_ref)
        s_idx = lax.broadcasted_iota(jnp.int32, (KEY_BLOCK, rows), 0)
        t_idx = lax.broadcasted_iota(jnp.int32, (KEY_BLOCK, rows), 1) // heads
        process([(kn[...], vn[...], s_idx < t_idx)])

    @pl.when(p > 0)
    def _():
        process([(k0_ref[...], v0_ref[...], None), (k1_ref[...], v1_ref[...], None)])

    @pl.when(p == n_steps)
    def _():
        _store_by_head(o_ref, acc[...], heads, tq)


def sb_sample_attention(qkv3, cache_k, cache_v, layer, page_table, bias):
    b, tq, d3 = qkv3.shape
    d = d3 // 3
    heads = d // HEAD_DIM
    rows = tq * heads
    n_pages = page_table.shape[1]
    page = cache_k.shape[2]
    assert page == KEY_BLOCK and tq <= KEY_BLOCK
    bias_rows = jnp.tile(bias.astype(F32), tq).reshape(1, rows)

    assert n_pages % 2 == 0
    n_steps = n_pages // 2

    def page_spec(slot):
        def page_map(bb, p, pt):
            return (layer, pt[bb, n_pages - 2 * jnp.maximum(p, 1) + 1 - slot], 0, 0, 0)
        return pl.BlockSpec((None, None, page, heads, HEAD_DIM), page_map)

    grid_spec = pltpu.PrefetchScalarGridSpec(
        num_scalar_prefetch=1,
        grid=(b, n_steps + 1),
        in_specs=[pl.BlockSpec((1, rows), lambda bb, p, pt: (0, 0)),
                  pl.BlockSpec((None, tq, d3), lambda bb, p, pt: (bb, 0, 0)),
                  page_spec(0), page_spec(1), page_spec(0), page_spec(1)],
        out_specs=pl.BlockSpec((None, tq, d), lambda bb, p, pt: (bb, 0, 0)),
        scratch_shapes=[pltpu.VMEM((rows, HEAD_DIM), F32),
                        pltpu.VMEM((KEY_BLOCK, heads, HEAD_DIM), F32),
                        pltpu.VMEM((KEY_BLOCK, heads, HEAD_DIM), F32),
                        pltpu.VMEM((rows, HEAD_DIM), F32),
                        pltpu.VMEM((1, rows), F32),
                        pltpu.VMEM((2, KEY_BLOCK, rows), F32)])
    return pl.pallas_call(
        functools.partial(_sb_sample_kernel, heads=heads, tq=tq, n_steps=n_steps),
        out_shape=jax.ShapeDtypeStruct((b, tq, d), F32),
        grid_spec=grid_spec,
        compiler_params=_params(("parallel", "arbitrary"), 48 << 20),
        name="sb_sample",
    )(page_table, bias_rows, qkv3, cache_k, cache_k, cache_v, cache_v)


def _s5_discretize(lam_re, lam_im, log_dt):
    dt = jnp.exp(log_dt)
    mag = jnp.exp(lam_re * dt)
    ab_re = mag * jnp.cos(lam_im * dt)
    ab_im = mag * jnp.sin(lam_im * dt)
    den = lam_re * lam_re + lam_im * lam_im
    nr = ab_re - 1.0
    ni = ab_im
    f_re = (nr * lam_re + ni * lam_im) / den
    f_im = (ni * lam_re - nr * lam_im) / den
    return ab_re, ab_im, f_re, f_im


def _s5_tile_mask():
    r = lax.broadcasted_iota(jnp.int32, (HEAD_DIM, S5_TILE_STATE), 0) // S5_GROUP
    c = lax.broadcasted_iota(jnp.int32, (HEAD_DIM, S5_TILE_STATE), 1) // S5_STATE
    return r == c


def _s5_bbar(bt_re, bt_im, f_re, f_im):
    m = _s5_tile_mask()
    bb_re = jnp.where(m, f_re * bt_re - f_im * bt_im, 0.0)
    bb_im = jnp.where(m, f_re * bt_im + f_im * bt_re, 0.0)
    return bb_re, bb_im


def _gelu_tanh(x):
    return 0.5 * x * (1.0 + jnp.tanh(0.7978845608028654 * (x + 0.044715 * (x * x * x))))


def _s5_prompt_kernel(u_ref, lre_ref, lim_ref, ldt_ref, btr_ref, bti_ref, cr_ref, ci_ref, d_ref,
                      y_ref, hre_ref, him_ref,
                      toep, bend, cpow, ucat, sloc, hprev, yacc, ys, *, batch, seq):
    nch = seq // S5_CHUNK
    ns = S5_TILE_STATE
    ab_re, ab_im, f_re, f_im = _s5_discretize(lre_ref[...], lim_ref[...], ldt_ref[...])
    bb_re, bb_im = _s5_bbar(btr_ref[...], bti_ref[...], f_re, f_im)
    m = _s5_tile_mask()
    c_re = jnp.where(m, cr_ref[...], 0.0)
    c_im = jnp.where(m, ci_ref[...], 0.0)
    c_cat = jnp.concatenate([c_re, -c_im], axis=1).astype(BF16)

    toep[...] = jnp.zeros_like(toep)
    p_re = jnp.ones_like(ab_re)
    p_im = jnp.zeros_like(ab_im)
    for tau in range(S5_CHUNK):
        bt_cat = jnp.concatenate([bb_re * p_re - bb_im * p_im, bb_re * p_im + bb_im * p_re],
                                 axis=1).astype(BF16)
        s_blk = S5_CHUNK - 1 - tau
        bend[s_blk * HEAD_DIM:(s_blk + 1) * HEAD_DIM, :] = bt_cat
        k_tau = lax.dot_general(bt_cat, c_cat, _NT, preferred_element_type=F32).astype(BF16)
        for s in range(S5_CHUNK - tau):
            t = s + tau
            toep[s * HEAD_DIM:(s + 1) * HEAD_DIM, t * HEAD_DIM:(t + 1) * HEAD_DIM] = k_tau
        p_re, p_im = p_re * ab_re - p_im * ab_im, p_re * ab_im + p_im * ab_re
        cpow[tau * HEAD_DIM:(tau + 1) * HEAD_DIM, :] = jnp.concatenate(
            [c_re * p_re - c_im * p_im, -(c_re * p_im + c_im * p_re)], axis=1).astype(BF16)
    a16_re, a16_im = p_re, p_im

    for b in range(batch):
        for s in range(S5_CHUNK):
            ucat[b * nch:(b + 1) * nch, s * HEAD_DIM:(s + 1) * HEAD_DIM] = (
                u_ref[b, pl.ds(s, nch, stride=S5_CHUNK), :].astype(BF16))
    uc = ucat[...]
    yacc[...] = jnp.dot(uc, toep[...], preferred_element_type=F32)
    sloc[...] = jnp.dot(uc, bend[...], preferred_element_type=F32)

    def step(n, carry):
        out = []
        for b in range(batch):
            h_re, h_im = carry[2 * b], carry[2 * b + 1]
            r = b * nch + n
            hprev[pl.ds(r, 1), 0:ns] = h_re
            hprev[pl.ds(r, 1), ns:2 * ns] = h_im
            s_re = sloc[pl.ds(r, 1), 0:ns]
            s_im = sloc[pl.ds(r, 1), ns:2 * ns]
            out.append(a16_re * h_re - a16_im * h_im + s_re)
            out.append(a16_re * h_im + a16_im * h_re + s_im)
        return tuple(out)

    zero = jnp.zeros((1, ns), F32)
    fin = lax.fori_loop(0, nch, step, (zero,) * (2 * batch))
    for b in range(batch):
        hre_ref[b:b + 1, :] = fin[2 * b]
        him_ref[b:b + 1, :] = fin[2 * b + 1]

    yacc[...] += lax.dot_general(hprev[...].astype(BF16), cpow[...], _NT,
                                 preferred_element_type=F32)
    dsk = d_ref[...]
    for b in range(batch):
        for t in range(S5_CHUNK):
            rows = pl.ds(t, nch, stride=S5_CHUNK)
            y = (yacc[b * nch:(b + 1) * nch, t * HEAD_DIM:(t + 1) * HEAD_DIM]
                 + dsk * u_ref[b, rows, :])
            ys[b, rows, :] = _gelu_tanh(y)
    y_ref[...] = ys[...].astype(y_ref.dtype)


def _s5_tile_params(a_re, a_im, log_dt, b_re, b_im, c_re, c_im, d_skip):
    g, p = a_re.shape
    nt = g // S5_TILE_GROUPS
    lre = a_re.astype(F32).reshape(nt, 1, S5_TILE_STATE)
    lim = a_im.astype(F32).reshape(nt, 1, S5_TILE_STATE)
    ldt = jnp.repeat(log_dt.astype(F32), p).reshape(nt, 1, S5_TILE_STATE)

    def rows_gc(x):
        x = x.astype(F32).reshape(nt, HEAD_DIM, p)
        return jnp.tile(x, (1, 1, S5_TILE_GROUPS))

    btr = rows_gc(jnp.swapaxes(b_re, 1, 2))
    bti = rows_gc(jnp.swapaxes(b_im, 1, 2))
    cr = rows_gc(c_re)
    ci = rows_gc(c_im)
    dd = d_skip.astype(F32).reshape(1, g * S5_GROUP)
    return lre, lim, ldt, btr, bti, cr, ci, dd


def _s5_param_specs(n_lead):
    lead = (lambda i: (i, 0, 0))
    row = pl.BlockSpec((None, 1, S5_TILE_STATE), lead)
    mat = pl.BlockSpec((None, HEAD_DIM, S5_TILE_STATE), lead)
    del n_lead
    return [row, row, row, mat, mat, mat, mat, pl.BlockSpec((1, HEAD_DIM), lambda i: (0, i))]


def s5_prompt(u, tile_params):
    b, t, d = u.shape
    nt = d // HEAD_DIM
    nch = t // S5_CHUNK
    assert t % S5_CHUNK == 0 and nch % 8 == 0
    wide = S5_CHUNK * HEAD_DIM
    ns2 = 2 * S5_TILE_STATE
    return pl.pallas_call(
        functools.partial(_s5_prompt_kernel, batch=b, seq=t),
        out_shape=(jax.ShapeDtypeStruct((b, t, d), BF16),
                   jax.ShapeDtypeStruct((b, nt * S5_TILE_STATE), F32),
                   jax.ShapeDtypeStruct((b, nt * S5_TILE_STATE), F32)),
        grid=(nt,),
        in_specs=[pl.BlockSpec((b, t, HEAD_DIM), lambda i: (0, 0, i))] + _s5_param_specs(1),
        out_specs=(pl.BlockSpec((b, t, HEAD_DIM), lambda i: (0, 0, i)),
                   pl.BlockSpec((b, S5_TILE_STATE), lambda i: (0, i)),
                   pl.BlockSpec((b, S5_TILE_STATE), lambda i: (0, i))),
        scratch_shapes=[pltpu.VMEM((wide, wide), BF16),
                        pltpu.VMEM((wide, ns2), BF16),
                        pltpu.VMEM((wide, ns2), BF16),
                        pltpu.VMEM((b * nch, wide), BF16),
                        pltpu.VMEM((b * nch, ns2), F32),
                        pltpu.VMEM((b * nch, ns2), F32),
                        pltpu.VMEM((b * nch, wide), F32),
                        pltpu.VMEM((b, t, HEAD_DIM), F32)],
        compiler_params=_params(("parallel",), 58 << 20),
        name="s5_prompt",
    )(u, *tile_params)


def _s5_sample_kernel(u_ref, h0r_ref, h0i_ref, lre_ref, lim_ref, ldt_ref, btr_ref, bti_ref,
                      cr_ref, ci_ref, d_ref, y_ref, hre_ref, him_ref, *, tq):
    ab_re, ab_im, f_re, f_im = _s5_discretize(lre_ref[...], lim_ref[...], ldt_ref[...])
    bb_re, bb_im = _s5_bbar(btr_ref[...], bti_ref[...], f_re, f_im)
    m = _s5_tile_mask()
    c_cat = jnp.concatenate([jnp.where(m, cr_ref[...], 0.0), -jnp.where(m, ci_ref[...], 0.0)],
                            axis=1).astype(BF16)
    b_cat = jnp.concatenate([bb_re, bb_im], axis=1).astype(BF16)
    ns = S5_TILE_STATE
    h_re = h0r_ref[...]
    h_im = h0i_ref[...]
    dsk = d_ref[...]
    for t in range(tq):
        u = u_ref[t]
        bu = jnp.dot(u.astype(BF16), b_cat, preferred_element_type=F32)
        h_re, h_im = (ab_re * h_re - ab_im * h_im + bu[:, 0:ns],
                      ab_re * h_im + ab_im * h_re + bu[:, ns:2 * ns])
        h_cat = jnp.concatenate([h_re, h_im], axis=1).astype(BF16)
        y = lax.dot_general(h_cat, c_cat, _NT, preferred_element_type=F32) + dsk * u
        y_ref[t] = _gelu_tanh(y)
    hre_ref[...] = h_re
    him_ref[...] = h_im


def s5_sample(u_tb, h0_re, h0_im, tile_params):
    tq, b, d = u_tb.shape
    nt = d // HEAD_DIM
    st = pl.BlockSpec((b, S5_TILE_STATE), lambda i: (0, i))
    ublk = pl.BlockSpec((tq, b, HEAD_DIM), lambda i: (0, 0, i))
    return pl.pallas_call(
        functools.partial(_s5_sample_kernel, tq=tq),
        out_shape=(jax.ShapeDtypeStruct((tq, b, d), F32),
                   jax.ShapeDtypeStruct((b, nt * S5_TILE_STATE), F32),
                   jax.ShapeDtypeStruct((b, nt * S5_TILE_STATE), F32)),
        grid=(nt,),
        in_specs=[ublk, st, st] + _s5_param_specs(1),
        out_specs=(ublk, st, st),
        compiler_params=_params(("parallel",)),
        name="s5_sample",
    )(u_tb, h0_re, h0_im, *tile_params)


def _band_prompt_kernel(*refs, seq, groups):
    ng = len(groups)
    qkv = refs[:3 * ng]
    o_ref = refs[3 * ng]
    og, lg = refs[3 * ng + 1], refs[3 * ng + 2]
    kb = KEY_BLOCK
    a1 = lax.broadcasted_iota(jnp.int32, (kb, kb), 0)
    c1 = lax.broadcasted_iota(jnp.int32, (kb, kb), 1)
    a2 = lax.broadcasted_iota(jnp.int32, (kb, 2 * kb), 0)
    c2 = lax.broadcasted_iota(jnp.int32, (kb, 2 * kb), 1)
    first_valid = c1 <= a1
    band_valid = (c2 >= a2) & (c2 <= a2 + kb)

    for g, (win, dil) in enumerate(groups):
        assert win // dil == kb and seq % (dil * kb) == 0
        q_ref, k_ref, v_ref = qkv[3 * g], qkv[3 * g + 1], qkv[3 * g + 2]
        nb = seq // (dil * kb)

        def rows(r, n):
            if dil == 1:
                return pl.ds(n * kb, kb)
            return pl.ds(r + n * kb * dil, kb, stride=dil)

        for r in range(dil):
            for n in range(nb):
                qb = q_ref[rows(r, n), :].astype(BF16)
                if n == 0:
                    kk = k_ref[rows(r, 0), :].astype(BF16)
                    vv = v_ref[rows(r, 0), :].astype(BF16)
                    valid = first_valid
                else:
                    kk = jnp.concatenate([k_ref[rows(r, n - 1), :], k_ref[rows(r, n), :]],
                                         axis=0).astype(BF16)
                    vv = jnp.concatenate([v_ref[rows(r, n - 1), :], v_ref[rows(r, n), :]],
                                         axis=0).astype(BF16)
                    valid = band_valid
                s = lax.dot_general(qb, kk, _NT, preferred_element_type=F32) * ATTN_SCALE
                s = jnp.where(valid, s, -jnp.inf)
                mx = jnp.max(s, axis=-1, keepdims=True)
                p = jnp.exp(s - mx)
                l = jnp.sum(p, axis=-1, keepdims=True)
                o = jnp.dot(p.astype(BF16), vv, preferred_element_type=F32) / l
                og[g, rows(r, n), :] = o
                lg[g, rows(r, n), :] = jnp.broadcast_to(mx + jnp.log(l), (kb, HEAD_DIM))

    lses = [lg[g] for g in range(ng)]
    top = functools.reduce(jnp.maximum, lses)
    es = [jnp.exp(x - top) for x in lses]
    num = functools.reduce(lambda a, b: a + b, [e * og[g] for g, e in enumerate(es)])
    den = functools.reduce(lambda a, b: a + b, es)
    o_ref[...] = (num / den).astype(o_ref.dtype)


def band_prompt_attention(qkv, batch, seq, heads, groups):
    ng = len(groups)
    d = heads * HEAD_DIM

    def spec(c):
        return pl.BlockSpec((seq, HEAD_DIM), lambda b, h: (b, c * heads + h))

    return pl.pallas_call(
        functools.partial(_band_prompt_kernel, seq=seq, groups=groups),
        out_shape=jax.ShapeDtypeStruct((batch * seq, d), BF16),
        grid=(batch, heads),
        in_specs=[spec(c) for c in range(3 * ng)],
        out_specs=pl.BlockSpec((seq, HEAD_DIM), lambda b, h: (b, h)),
        scratch_shapes=[pltpu.VMEM((ng, seq, HEAD_DIM), F32),
                        pltpu.VMEM((ng, seq, HEAD_DIM), F32)],
        compiler_params=_params(("parallel", "parallel"), 48 << 20),
        name="band_prompt",
    )(*([qkv] * (3 * ng)))


def _band_sample_kernel(qkv_ref, kb_ref, vb_ref, o_ref, lse_ref, qd, kn, vn, acc, mx_ref, l_ref,
                        z_ref, *, heads, tq, group, dil, n_buf):
    c = pl.program_id(1)
    d = heads * HEAD_DIM
    rows = tq * heads
    key = lax.broadcasted_iota(jnp.int32, (KEY_BLOCK, rows), 0)
    tok = lax.broadcasted_iota(jnp.int32, (KEY_BLOCK, rows), 1) // heads
    if dil == 1:
        new_valid = key <= tok
        buf_valid = key >= tok
    else:
        new_valid = key == tok
        buf_valid = tok == c - 1

    def process(k3, v3, valid):
        s = _key_logits(k3, qd[...].astype(BF16), heads, z_ref) * ATTN_SCALE
        s = jnp.where(valid, s, -jnp.inf)
        m_old = mx_ref[...]
        m_new = jnp.maximum(m_old, jnp.max(s, axis=0, keepdims=True))
        p = jnp.exp(s - m_new)
        alpha = jnp.exp(m_old - m_new)
        l_ref[...] = alpha * l_ref[...] + jnp.sum(p, axis=0, keepdims=True)
        mx_ref[...] = m_new
        acc[...] = _lanes_to_rows(alpha) * acc[...] + _weighted_values(p, v3, heads)

    @pl.when(c == 0)
    def _():
        base = group * 3 * d
        _fill_queries(qd, qkv_ref[:, base:base + d], heads, tq)
        _fill_new_rows(kn, qkv_ref[:, base + d:base + 2 * d], heads, tq)
        _fill_new_rows(vn, qkv_ref[:, base + 2 * d:base + 3 * d], heads, tq)
        acc[...] = jnp.zeros_like(acc)
        mx_ref[...] = jnp.full_like(mx_ref, NEG_BIG)
        l_ref[...] = jnp.zeros_like(l_ref)
        process(kn[...], vn[...], new_valid)

    @pl.when(c > 0)
    def _():
        process(kb_ref[...], vb_ref[...], buf_valid)

    @pl.when(c == n_buf)
    def _():
        l = l_ref[...]
        _store_by_head(o_ref, acc[...] / _lanes_to_rows(l), heads, tq)
        _store_by_head(lse_ref, _lanes_to_rows(mx_ref[...] + jnp.log(l)), heads, tq)


def band_sample_attention(qkv3, buf_k, buf_v, layer, group, win, dil):
    b, tq, dall = qkv3.shape
    nc, _, wb, heads, hd = buf_k.shape
    d = heads * hd
    assert wb == win and win // dil == KEY_BLOCK and tq <= KEY_BLOCK
    if dil == 1:
        kv = buf_k, buf_v
        n_buf = 1
        last = wb // KEY_BLOCK - 1
        blk = pl.BlockSpec((None, None, KEY_BLOCK, heads, hd),
                           lambda bb, c: (layer, bb, last, 0, 0))
    else:
        assert tq <= dil
        shape = (nc, b, wb // dil, dil, heads, hd)
        kv = buf_k.reshape(shape), buf_v.reshape(shape)
        n_buf = tq
        blk = pl.BlockSpec((None, None, KEY_BLOCK, None, heads, hd),
                           lambda bb, c: (layer, bb, 0, jnp.maximum(c - 1, 0), 0, 0))
    rows = tq * heads
    out = pl.BlockSpec((None, tq, d), lambda bb, c: (bb, 0, 0))
    return pl.pallas_call(
        functools.partial(_band_sample_kernel, heads=heads, tq=tq, group=group, dil=dil,
                          n_buf=n_buf),
        out_shape=(jax.ShapeDtypeStruct((b, tq, d), F32), jax.ShapeDtypeStruct((b, tq, d), F32)),
        grid=(b, n_buf + 1),
        in_specs=[pl.BlockSpec((None, tq, dall), lambda bb, c: (bb, 0, 0)), blk, blk],
        out_specs=(out, out),
        scratch_shapes=[pltpu.VMEM((rows, HEAD_DIM), F32),
                        pltpu.VMEM((KEY_BLOCK, heads, HEAD_DIM), F32),
                        pltpu.VMEM((KEY_BLOCK, heads, HEAD_DIM), F32),
                        pltpu.VMEM((rows, HEAD_DIM), F32),
                        pltpu.VMEM((1, rows), F32),
                        pltpu.VMEM((1, rows), F32),
                        pltpu.VMEM((KEY_BLOCK, rows), F32)],
        compiler_params=_params(("parallel", "arbitrary"), 40 << 20),
        name="band_sample",
    )(qkv3, *kv)


def _merge_groups_kernel(*refs, ng):
    outs, lses, o_ref = refs[:ng], refs[ng:2 * ng], refs[2 * ng]
    ls = [r[...] for r in lses]
    top = functools.reduce(jnp.maximum, ls)
    es = [jnp.exp(x - top) for x in ls]
    num = functools.reduce(lambda a, b: a + b, [e * r[...] for e, r in zip(es, outs)])
    den = functools.reduce(lambda a, b: a + b, es)
    o_ref[...] = (num / den).astype(o_ref.dtype)


def merge_groups(outs, lses):
    m, d = outs[0].shape
    return pl.pallas_call(
        functools.partial(_merge_groups_kernel, ng=len(outs)),
        out_shape=jax.ShapeDtypeStruct((m, d), BF16),
        name="merge_groups",
    )(*outs, *lses)


def _ffn(xp, xs, layer, norm_ffn, ffn_w1, ffn_w2):
    hp = rmsnorm(xp, norm_ffn, layer)
    hs = rmsnorm(xs, norm_ffn, layer)
    hid_p, hid_s = matmul(hp, hs, [ffn_w1], layer, ffn_w1.shape[-1], epi="relu2", out_dtype=BF16,
                          **_WIDE_TILE)
    return matmul_ktiled(hid_p, hid_s, ffn_w2, layer, xp, xs)


def _qkv_gain(q_gain, k_gain, d):
    reps = d // HEAD_DIM
    return jnp.concatenate([jnp.tile(q_gain.astype(F32), reps), jnp.tile(k_gain.astype(F32), reps),
                            jnp.ones((d,), F32)])


def kernel(x_prompt, x_sample, cache_a_k, cache_a_v, state_b_re, state_b_im, cache_c_k_w128, cache_c_v_w128, cache_c_k_w512, cache_c_v_w512, cache_c_k_w2048, cache_c_v_w2048, page_table, norm_mix, norm_ffn, a_w_qkv, a_q_gain, a_k_gain, a_logit_bias, a_w_o, b_w_in, b_a_re, b_a_im, b_log_dt, b_b_re, b_b_im, b_c_re, b_c_im, b_d, b_w_glu, b_w_gate, c_w_qkv, c_q_gain, c_k_gain, c_w_o, ffn_w1, ffn_w2):
    bp, tp, d = x_prompt.shape
    bs, ts, _ = x_sample.shape
    heads = d // HEAD_DIM
    depth = norm_mix.shape[0]
    c_buf_k = (cache_c_k_w128, cache_c_k_w512, cache_c_k_w2048)
    c_buf_v = (cache_c_v_w128, cache_c_v_w512, cache_c_v_w2048)
    ng = len(C_GROUPS)
    xp = x_prompt.reshape(bp * tp, d)
    xs = x_sample.reshape(bs * ts, d)
    ak_p, av_p, ak_s, av_s = [], [], [], []
    br_p, bi_p, br_s, bi_s = [], [], [], []
    ck_p = [[] for _ in C_GROUPS]
    cv_p = [[] for _ in C_GROUPS]
    ck_s = [[] for _ in C_GROUPS]
    cv_s = [[] for _ in C_GROUPS]
    ia = ib = ic = 0
    for layer in range(depth):
        hp = rmsnorm(xp, norm_mix, layer)
        hs = rmsnorm(xs, norm_mix, layer)
        kind = layer % N_MIXERS
        if kind == 0:
            gain = _qkv_gain(a_q_gain[ia], a_k_gain[ia], d).reshape(1, 3 * d)
            qkv_p, qkv_s = matmul(hp, hs, [a_w_qkv], ia, 3 * d, epi="headnorm", gain=gain,
                                  **_WIDE_TILE)
            op = sb_prompt_attention(qkv_p, a_logit_bias[ia].astype(F32), bp, tp, heads)
            os_ = sb_sample_attention(qkv_s.reshape(bs, ts, 3 * d), cache_a_k, cache_a_v, ia,
                                      page_table, a_logit_bias[ia])
            xp, xs = matmul(op, os_.reshape(bs * ts, d).astype(BF16), [a_w_o], ia, d,
                            res=xp, res_s=xs)
            ak_p.append(qkv_p[:, d:2 * d].reshape(bp, tp, heads, HEAD_DIM))
            av_p.append(qkv_p[:, 2 * d:].reshape(bp, tp, heads, HEAD_DIM))
            ak_s.append(qkv_s[:, d:2 * d].reshape(bs, ts, heads, HEAD_DIM))
            av_s.append(qkv_s[:, 2 * d:].reshape(bs, ts, heads, HEAD_DIM))
            ia += 1
        elif kind == 1:
            tile_params = _s5_tile_params(b_a_re[ib], b_a_im[ib], b_log_dt[ib], b_b_re[ib],
                                          b_b_im[ib], b_c_re[ib], b_c_im[ib], b_d[ib])
            up, us = matmul(hp, hs, [b_w_in], ib, d, **_WIDE_TILE)
            yp, rp, ip = s5_prompt(up.reshape(bp, tp, d), tile_params)
            us_tb = jnp.swapaxes(us.reshape(bs, ts, d), 0, 1)
            ys_tb, rn, im_ = s5_sample(us_tb, state_b_re[ib].reshape(bs, -1),
                                       state_b_im[ib].reshape(bs, -1), tile_params)
            ys = jnp.swapaxes(ys_tb, 0, 1).reshape(bs * ts, d).astype(BF16)
            xp, xs = matmul(yp.reshape(bp * tp, d), ys, [b_w_glu, b_w_gate], ib, d, epi="glu",
                            res=xp, res_s=xs, bm=512, bn=512)
            g = d // S5_GROUP
            br_p.append(rp.reshape(bp, g, S5_STATE))
            bi_p.append(ip.reshape(bp, g, S5_STATE))
            br_s.append(rn.reshape(bs, g, S5_STATE))
            bi_s.append(im_.reshape(bs, g, S5_STATE))
            ib += 1
        else:
            gain = jnp.concatenate([_qkv_gain(c_q_gain[ic, gi], c_k_gain[ic, gi], d)
                                    for gi in range(ng)]).reshape(1, ng * 3 * d)
            qkv_p, qkv_s = matmul(hp, hs, [c_w_qkv], ic, ng * 3 * d, epi="headnorm", gain=gain,
                                  **_WIDE_TILE)
            op = band_prompt_attention(qkv_p, bp, tp, heads, C_GROUPS)
            qkv_s3 = qkv_s.reshape(bs, ts, ng * 3 * d)
            outs, lses = [], []
            for gi, (win, dil) in enumerate(C_GROUPS):
                o_g, l_g = band_sample_attention(qkv_s3, c_buf_k[gi], c_buf_v[gi], ic, gi, win, dil)
                outs.append(o_g.reshape(bs * ts, d))
                lses.append(l_g.reshape(bs * ts, d))
            os_ = merge_groups(outs, lses)
            xp, xs = matmul(op, os_, [c_w_o], ic, d, res=xp, res_s=xs)
            qp5 = qkv_p.reshape(bp, tp, ng, 3, heads, HEAD_DIM)
            qs5 = qkv_s.reshape(bs, ts, ng, 3, heads, HEAD_DIM)
            for gi, (win, dil) in enumerate(C_GROUPS):
                keep = min(win, tp)
                ck_p[gi].append(qp5[:, tp - keep:, gi, 1])
                cv_p[gi].append(qp5[:, tp - keep:, gi, 2])
                wb = c_buf_k[gi].shape[2]
                ck_s[gi].append(jnp.concatenate([c_buf_k[gi][ic], qs5[:, :, gi, 1]], axis=1)[:, -wb:])
                cv_s[gi].append(jnp.concatenate([c_buf_v[gi][ic], qs5[:, :, gi, 2]], axis=1)[:, -wb:])
            ic += 1
        xp, xs = _ffn(xp, xs, layer, norm_ffn, ffn_w1, ffn_w2)
    return (xp.reshape(bp, tp, d), xs.reshape(bs, ts, d),
            jnp.stack(ak_p), jnp.stack(av_p), jnp.stack(ak_s), jnp.stack(av_s),
            jnp.stack(br_p), jnp.stack(bi_p), jnp.stack(br_s), jnp.stack(bi_s),
            jnp.stack(ck_p[0]), jnp.stack(cv_p[0]), jnp.stack(ck_p[1]), jnp.stack(cv_p[1]),
            jnp.stack(ck_p[2]), jnp.stack(cv_p[2]),
            jnp.stack(ck_s[0]), jnp.stack(cv_s[0]), jnp.stack(ck_s[1]), jnp.stack(cv_s[1]),
            jnp.stack(ck_s[2]), jnp.stack(cv_s[2]))
```
